```python
import math, functools
import jax, jax.numpy as jnp
from jax import lax
import numpy as np

D_MODEL = 1024
BATCH = 8
SEQ = 2048
DEPTH = 1
DEC_BATCH = 128
DEC_SEQ = 1
PAST_LEN = 2048
PAGE_SIZE = 128

PLE_DIM = 256
ATT_WIDTH = D_MODEL // 2
REC_WIDTH = D_MODEL - ATT_WIDTH
ATT_HEADS = 4
ATT_HD = ATT_WIDTH // (2 * ATT_HEADS)
REC_HEADS = 4
REC_DK = REC_WIDTH // REC_HEADS
REC_DV = REC_WIDTH // REC_HEADS
IN_WIDTH = 3 * ATT_WIDTH + 4 * REC_WIDTH
D_FF = 4 * D_MODEL
CHUNK = 64
Q_BLOCK = 128
EPS = 1e-6

kernel_name = "hymba_diffattn_hgrn2_step"


def rmsnorm(x, g):
    xf = x.astype(jnp.float32)
    y = xf * lax.rsqrt(jnp.mean(xf * xf, axis=-1, keepdims=True) + EPS)
    return (y * g.astype(jnp.float32)).astype(x.dtype)


def lambda_init_fn(li):
    return 0.8 - 0.6 * math.exp(-0.3 * li)


def diff_attend(q, k, v, q_pos, k_pos, lam):
    s = jnp.einsum('bqhmd,bkhmd->bhmqk', q, k).astype(jnp.float32) * (ATT_HD ** -0.5)
    mask = k_pos[None, :] <= q_pos[:, None]
    s = jnp.where(mask, s, -jnp.inf)
    a = jax.nn.softmax(s, axis=-1)
    w = a[:, :, 0] - lam * a[:, :, 1]
    return jnp.einsum('bhqk,bkhe->bqhe', w.astype(v.dtype), v)


def prompt_attn(q, k, v, lam):
    B, S = q.shape[:2]
    nb = S // Q_BLOCK
    qb = q.reshape(B, nb, Q_BLOCK, ATT_HEADS, 2, ATT_HD).transpose(1, 0, 2, 3, 4, 5)
    pb = jnp.arange(S).reshape(nb, Q_BLOCK)
    kk = k.reshape(B, S, ATT_HEADS, 2, ATT_HD)
    kpos = jnp.arange(S)
    ob = lax.map(lambda a: diff_attend(a[0], kk, v, a[1], kpos, lam), (qb, pb))
    return ob.transpose(1, 0, 2, 3, 4).reshape(B, S, ATT_HEADS, 2 * ATT_HD)


def sample_attn(q, k, v, lam, ck, cv, page_table):
    B, T = q.shape[:2]
    past_k = ck[page_table].reshape(B, -1, ATT_HEADS, 2 * ATT_HD)
    past_v = cv[page_table].reshape(B, -1, ATT_HEADS, 2 * ATT_HD)
    past = past_k.shape[1]
    kk = jnp.concatenate([past_k.astype(k.dtype), k], axis=1).reshape(B, past + T, ATT_HEADS, 2, ATT_HD)
    vv = jnp.concatenate([past_v.astype(v.dtype), v], axis=1)
    q_pos = past + jnp.arange(T)
    k_pos = jnp.arange(past + T)
    q = q.reshape(B, T, ATT_HEADS, 2, ATT_HD)
    return diff_attend(q, kk, vv, q_pos, k_pos, lam)


def hgrn_scan(q, k, v, logf, s0):
    B, T, H, _ = q.shape
    c = min(CHUNK, T)
    n = -(-T // c)
    pad = n * c - T
    if pad:
        padw = ((0, 0), (0, pad), (0, 0), (0, 0))
        q = jnp.pad(q, padw); k = jnp.pad(k, padw); v = jnp.pad(v, padw); logf = jnp.pad(logf, padw)

    def to_chunks(a):
        return a.reshape(B, n, c, H, a.shape[-1]).transpose(1, 0, 3, 2, 4)

    causal = jnp.tril(jnp.ones((c, c), dtype=bool))[:, :, None]

    def step(S, inp):
        qc, kc, vc, lc = inp
        b = jnp.cumsum(lc, axis=2)
        o_inter = jnp.einsum('bhtd,bhde->bhte', qc * jnp.exp(b), S)
        diff = b[:, :, :, None, :] - b[:, :, None, :, :]
        decay = jnp.exp(jnp.where(causal, diff, -jnp.inf))
        att = jnp.einsum('bhtd,bhsd,bhtsd->bhts', qc, kc, decay)
        o = o_inter + jnp.einsum('bhts,bhse->bhte', att, vc)
        b_last = b[:, :, -1:, :]
        S_new = jnp.exp(b_last[:, :, 0, :])[..., None] * S + jnp.einsum(
            'bhsd,bhse->bhde', kc * jnp.exp(b_last - b), vc)
        return S_new, o

    s_fin, o = lax.scan(step, s0, (to_chunks(q), to_chunks(k), to_chunks(v), to_chunks(logf)))
    o = o.transpose(1, 0, 3, 2, 4).reshape(B, n * c, H, v.shape[-1])[:, :T]
    return o, s_fin


def hgrn_mixer(rq, rf, ri, rg, lb, g_rec, s0):
    B, T, _ = rq.shape
    f32 = jnp.float32
    shp = (B, T, REC_HEADS, REC_DK)
    q = jax.nn.silu(rq.astype(f32)).reshape(shp)
    f = lb + (1.0 - lb) * jax.nn.sigmoid(rf.astype(f32))
    logf = jnp.log(f).reshape(shp)
    k = (1.0 - f).reshape(shp)
    v = ri.astype(f32).reshape(B, T, REC_HEADS, REC_DV)
    o, s_fin = hgrn_scan(q, k, v, logf, s0.astype(f32))
    o = rmsnorm(o.reshape(B, T, REC_WIDTH), g_rec) * jax.nn.silu(rg.astype(f32))
    return o.astype(rq.dtype), s_fin


def block(x, p_i, li, attend, s0, w_in, lambda_q1, lambda_k1, lambda_q2, lambda_k2,
          g_subln, hgrn_lb, g_rec, w_out, g_mix, g_ffn, w_up, w_down, w_ple_gate, w_ple_proj):
    B, T, _ = x.shape
    h = rmsnorm(x, g_mix[li])
    z = h @ w_in[li]
    cuts = [ATT_WIDTH, 2 * ATT_WIDTH, 3 * ATT_WIDTH, 3 * ATT_WIDTH + REC_WIDTH,
            3 * ATT_WIDTH + 2 * REC_WIDTH, 3 * ATT_WIDTH + 3 * REC_WIDTH]
    aq, ak, av, rq, rf, ri, rg = jnp.split(z, cuts, axis=-1)
    lam_init = lambda_init_fn(li)
    f32 = jnp.float32
    lam = (jnp.exp(jnp.sum(lambda_q1[li].astype(f32) * lambda_k1[li].astype(f32)))
           - jnp.exp(jnp.sum(lambda_q2[li].astype(f32) * lambda_k2[li].astype(f32))) + lam_init)
    q = aq.reshape(B, T, ATT_HEADS, 2, ATT_HD)
    k = ak.reshape(B, T, ATT_HEADS, 2 * ATT_HD)
    v = av.reshape(B, T, ATT_HEADS, 2 * ATT_HD)
    att = attend(q, k, v, lam)
    att = (rmsnorm(att, g_subln[li]) * (1.0 - lam_init)).reshape(B, T, ATT_WIDTH)
    lb = jnp.cumsum(jax.nn.softmax(hgrn_lb.astype(f32), axis=0), axis=0)[li]
    rec, s_new = hgrn_mixer(rq, rf, ri, rg, lb, g_rec[li], s0)
    mix = jnp.concatenate([att.astype(x.dtype), rec.astype(x.dtype)], axis=-1) @ w_out[li]
    x = x + mix
    hf = rmsnorm(x, g_ffn[li]) @ w_up[li]
    x = x + jnp.square(jax.nn.relu(hf)) @ w_down[li]
    x = x + jax.nn.sigmoid(x @ w_ple_gate[li]) * (p_i @ w_ple_proj[li])
    return x, k, v, s_new


def setup_inputs(seed: int = 0) -> dict:
    key = jax.random.key(seed)
    ks = jax.random.split(key, 26)
    f32 = jnp.float32
    n_pages = PAST_LEN // PAGE_SIZE
    n_pool = (DEC_BATCH * n_pages * 5) // 4

    def nrm(k, shape, scale):
        return jax.random.normal(k, shape, f32) * scale

    page_table = jax.random.permutation(ks[5], n_pool)[:DEC_BATCH * n_pages]
    page_table = page_table.reshape(DEC_BATCH, n_pages).astype(jnp.int32)
    return {
        'x_prompt': nrm(ks[0], (BATCH, SEQ, D_MODEL), 1.0),
        'x_sample': nrm(ks[1], (DEC_BATCH, DEC_SEQ, D_MODEL), 1.0),
        'cache_k': nrm(ks[2], (DEPTH, n_pool, PAGE_SIZE, ATT_HEADS, 2 * ATT_HD), 1.0),
        'cache_v': nrm(ks[3], (DEPTH, n_pool, PAGE_SIZE, ATT_HEADS, 2 * ATT_HD), 1.0),
        'state_hgrn': nrm(ks[4], (DEPTH, DEC_BATCH, REC_HEADS, REC_DK, REC_DV), 0.2),
        'page_table': page_table,
        'p_prompt': nrm(ks[6], (DEPTH, BATCH, SEQ, PLE_DIM), 1.0),
        'p_sample': nrm(ks[7], (DEPTH, DEC_BATCH, DEC_SEQ, PLE_DIM), 1.0),
        'w_in': nrm(ks[8], (DEPTH, D_MODEL, IN_WIDTH), D_MODEL ** -0.5),
        'lambda_q1': nrm(ks[9], (DEPTH, ATT_HD), 0.1),
        'lambda_k1': nrm(ks[10], (DEPTH, ATT_HD), 0.1),
        'lambda_q2': nrm(ks[11], (DEPTH, ATT_HD), 0.1),
        'lambda_k2': nrm(ks[12], (DEPTH, ATT_HD), 0.1),
        'g_subln': 1.0 + nrm(ks[13], (DEPTH, 2 * ATT_HD), 0.02),
        'hgrn_lb': nrm(ks[14], (DEPTH + 1, REC_WIDTH), 0.1),
        'g_rec': 1.0 + nrm(ks[15], (DEPTH, REC_WIDTH), 0.02),
        'w_out': nrm(ks[16], (DEPTH, D_MODEL, D_MODEL), D_MODEL ** -0.5),
        'g_mix': 1.0 + nrm(ks[17], (DEPTH, D_MODEL), 0.02),
        'g_ffn': 1.0 + nrm(ks[18], (DEPTH, D_MODEL), 0.02),
        'w_up': nrm(ks[19], (DEPTH, D_MODEL, D_FF), D_MODEL ** -0.5),
        'w_down': nrm(ks[20], (DEPTH, D_FF, D_MODEL), D_FF ** -0.5),
        'w_ple_gate': nrm(ks[21], (DEPTH, D_MODEL, D_MODEL), D_MODEL ** -0.5),
        'w_ple_proj': nrm(ks[22], (DEPTH, PLE_DIM, D_MODEL), PLE_DIM ** -0.5),
        'g_final': 1.0 + nrm(ks[23], (D_MODEL,), 0.02),
    }


def reference(x_prompt, x_sample, cache_k, cache_v, state_hgrn, page_table, p_prompt, p_sample,
              w_in, lambda_q1, lambda_k1, lambda_q2, lambda_k2, g_subln, hgrn_lb, g_rec, w_out,
              g_mix, g_ffn, w_up, w_down, w_ple_gate, w_ple_proj, g_final):
    weights = (w_in, lambda_q1, lambda_k1, lambda_q2, lambda_k2, g_subln, hgrn_lb, g_rec, w_out,
               g_mix, g_ffn, w_up, w_down, w_ple_gate, w_ple_proj)
    hp, hs = x_prompt, x_sample
    kp_l, vp_l, sp_l, ks_l, vs_l, ss_l = [], [], [], [], [], []
    for li in range(DEPTH):
        s0_p = jnp.zeros((hp.shape[0], REC_HEADS, REC_DK, REC_DV), jnp.float32)
        hp, kp, vp, sp = block(hp, p_prompt[li], li, prompt_attn, s0_p, *weights)
        attend_s = functools.partial(sample_attn, ck=cache_k[li], cv=cache_v[li], page_table=page_table)
        hs, ksm, vsm, ssm = block(hs, p_sample[li], li, attend_s, state_hgrn[li], *weights)
        kp_l.append(kp); vp_l.append(vp); sp_l.append(sp)
        ks_l.append(ksm); vs_l.append(vsm); ss_l.append(ssm)
    y_prompt = rmsnorm(hp, g_final)
    y_sample = rmsnorm(hs, g_final)
    k_prompt = jnp.stack(kp_l, 0)
    v_prompt = jnp.stack(vp_l, 0)
    s_prompt = jnp.stack(sp_l, 0)
    k_sample = jnp.stack(ks_l, 0)
    v_sample = jnp.stack(vs_l, 0)
    s_sample = jnp.stack(ss_l, 0)
    return (y_prompt, y_sample, k_prompt, v_prompt, s_prompt, k_sample, v_sample, s_sample)
```

```python
import functools
import math

import jax
import jax.numpy as jnp
import numpy as np
from jax import lax
from jax.experimental import pallas as pl
from jax.experimental.pallas import tpu as pltpu

F32 = jnp.float32
BF16 = jnp.bfloat16

D_MODEL = 1024
ATT_HEADS = 4
ATT_HD = 64
HEAD_W = 2 * ATT_HD
ATT_WIDTH = ATT_HEADS * HEAD_W
REC_HEADS = 4
REC_DK = 128
REC_WIDTH = REC_HEADS * REC_DK
N_PROJ = 7
PLE_DIM = 256
D_FF = 4 * D_MODEL
EPS = 1e-6
ATT_SCALE = ATT_HD ** -0.5

V7X_VMEM_BYTES = 64 * 1024 * 1024
VMEM_LIMIT = 52 * 1024 * 1024

HGRN_CHUNK = 64
HGRN_CUM_BLOCK = 256
HGRN_LEVELS = (8, 16, 32)
BAND = 8


def _lambda_init(li):
    return 0.8 - 0.6 * math.exp(-0.3 * li)


def _rms(x, g):
    return x * lax.rsqrt(jnp.mean(x * x, axis=-1, keepdims=True) + EPS) * g


def _silu(x):
    return x * jax.nn.sigmoid(x)


def _nt_dot(a, b):
    return lax.dot_general(a, b, (((1,), (1,)), ((), ())), preferred_element_type=F32)


def _tn_dot(a, b):
    return lax.dot_general(a, b, (((0,), (0,)), ((), ())), preferred_element_type=F32)


def _lam_from(lams, li):
    a = jnp.sum(lams[0:1, :] * lams[1:2, :], axis=-1, keepdims=True)
    b = jnp.sum(lams[2:3, :] * lams[3:4, :], axis=-1, keepdims=True)
    return jnp.exp(a) - jnp.exp(b) + _lambda_init(li)


def _inproj_kernel(x_ref, g_ref, w_ref, lb_ref,
                   aq_ref, ak_ref, av_ref, akb_ref, avb_ref,
                   rq_ref, lf_ref, rk_ref, rv_ref, rg_ref, *, li):
    h = _rms(x_ref[...], g_ref[...]).astype(BF16)

    def proj(i):
        return jnp.dot(h, w_ref[:, i * ATT_WIDTH:(i + 1) * ATT_WIDTH], preferred_element_type=F32)

    aq_ref[...] = (proj(0) * ATT_SCALE).astype(BF16)
    k = proj(1)
    ak_ref[...] = k
    akb_ref[...] = k.astype(BF16)
    v = proj(2)
    av_ref[...] = v
    avb_ref[...] = v.astype(BF16)
    rq_ref[...] = _silu(proj(3)).astype(BF16)
    lbp = lb_ref[...]
    e = jnp.exp(lbp - jnp.max(lbp, axis=0, keepdims=True))
    lb = jnp.sum(e[:li + 1], axis=0, keepdims=True) / jnp.sum(e, axis=0, keepdims=True)
    f = lb + (1.0 - lb) * jax.nn.sigmoid(proj(4))
    lf_ref[...] = jnp.log(f)
    rk_ref[...] = (1.0 - f).astype(BF16)
    rv_ref[...] = proj(5).astype(BF16)
    rg_ref[...] = _silu(proj(6)).astype(BF16)


def _inproj(x, g_mix, w_in, hgrn_lb, li, tm):
    n = x.shape[0]
    grid = (n // tm,)
    row = lambda i: (i, 0)
    const = lambda i: (0, 0)
    wide = pl.BlockSpec((tm, ATT_WIDTH), row)
    out_dtypes = [BF16, F32, F32, BF16, BF16, BF16, F32, BF16, BF16, BF16]
    return pl.pallas_call(
        functools.partial(_inproj_kernel, li=li),
        grid=grid,
        in_specs=[pl.BlockSpec((tm, D_MODEL), row),
                  pl.BlockSpec((1, D_MODEL), const),
                  pl.BlockSpec(w_in.shape, const),
                  pl.BlockSpec(hgrn_lb.shape, const)],
        out_specs=[wide] * len(out_dtypes),
        out_shape=[jax.ShapeDtypeStruct((n, ATT_WIDTH), dt) for dt in out_dtypes],
        compiler_params=pltpu.CompilerParams(dimension_semantics=("arbitrary",),
                                             vmem_limit_bytes=VMEM_LIMIT),
        name="inproj",
    )(x, g_mix, w_in, hgrn_lb)


def _attn_prompt_kernel(lams_ref, q_ref, k_ref, v_ref, g_ref, o_ref, *, li, tq):
    qi = pl.program_id(2)
    q = q_ref[...]
    lane = lax.broadcasted_iota(jnp.int32, q.shape, 1)
    zero = jnp.zeros_like(q)
    qs = jnp.concatenate([jnp.where(lane < ATT_HD, q, zero),
                          jnp.where(lane >= ATT_HD, q, zero)], axis=0)

    def step(j, carry, diagonal):
        m, l, acc = carry
        kb = k_ref[pl.ds(pl.multiple_of(j * tq, tq), tq), :]
        vb = v_ref[pl.ds(pl.multiple_of(j * tq, tq), tq), :]
        s = _nt_dot(qs, kb)
        if diagonal:
            r = lax.broadcasted_iota(jnp.int32, s.shape, 0)
            c = lax.broadcasted_iota(jnp.int32, s.shape, 1)
            r = jnp.where(r >= tq, r - tq, r)
            s = jnp.where(c <= r, s, -jnp.inf)
        m_new = jnp.maximum(m, jnp.max(s, axis=-1, keepdims=True))
        alpha = jnp.exp(m - m_new)
        p = jnp.exp(s - m_new)
        l = alpha * l + jnp.sum(p, axis=-1, keepdims=True)
        acc = alpha * acc + jnp.dot(p.astype(BF16), vb, preferred_element_type=F32)
        return m_new, l, acc

    init = (jnp.full((2 * tq, 1), -jnp.inf, F32), jnp.zeros((2 * tq, 1), F32),
            jnp.zeros((2 * tq, HEAD_W), F32))
    carry = lax.fori_loop(0, qi, lambda j, c: step(j, c, False), init)
    _, l, acc = step(qi, carry, True)
    o = acc / l
    lam = _lam_from(lams_ref[...], li)
    d = o[:tq] - lam * o[tq:]
    o_ref[...] = (_rms(d, g_ref[...]) * (1.0 - _lambda_init(li))).astype(BF16)


def _attn_prompt(lams, aq, akb, avb, g_subln, li, batch, seq, tq):
    nq = seq // tq
    n = batch * seq
    return pl.pallas_call(
        functools.partial(_attn_prompt_kernel, li=li, tq=tq),
        grid=(batch, ATT_HEADS, nq),
        in_specs=[pl.BlockSpec(lams.shape, lambda b, h, i: (0, 0)),
                  pl.BlockSpec((tq, HEAD_W), lambda b, h, i: (b * nq + i, h)),
                  pl.BlockSpec((seq, HEAD_W), lambda b, h, i: (b, h)),
                  pl.BlockSpec((seq, HEAD_W), lambda b, h, i: (b, h)),
                  pl.BlockSpec((1, HEAD_W), lambda b, h, i: (0, 0))],
        out_specs=pl.BlockSpec((tq, HEAD_W), lambda b, h, i: (b * nq + i, h)),
        out_shape=jax.ShapeDtypeStruct((n, ATT_WIDTH), BF16),
        compiler_params=pltpu.CompilerParams(
            dimension_semantics=("arbitrary", "arbitrary", "arbitrary"),
            vmem_limit_bytes=VMEM_LIMIT),
        name="attn_prompt",
    )(lams, aq, akb, avb, g_subln)


SCORE_ROWS = 16


def _attn_sample_kernel(pt_ref, lams_ref, q_ref, kn_ref, vn_ref, g_ref, *rest, li, n_pages, page):
    del pt_ref
    k_pages = rest[:n_pages]
    v_pages = rest[n_pages:2 * n_pages]
    o_ref = rest[2 * n_pages]
    q = q_ref[0].astype(F32)
    shape = (SCORE_ROWS, ATT_WIDTH)
    row = lax.broadcasted_iota(jnp.int32, shape, 0)
    col = lax.broadcasted_iota(jnp.int32, shape, 1)
    qrows = jnp.where(col // ATT_HD == row, jnp.broadcast_to(q, shape), 0.0)
    qb = qrows.astype(BF16)
    s_past = jnp.concatenate([_nt_dot(qb, kp[0].astype(BF16)) for kp in k_pages], axis=1)
    s_new = jnp.sum(qrows * kn_ref[0], axis=-1, keepdims=True)
    m = jnp.maximum(jnp.max(s_past, axis=-1, keepdims=True), s_new)
    p = jnp.exp(s_past - m)
    p_new = jnp.exp(s_new - m)
    l = jnp.sum(p, axis=-1, keepdims=True) + p_new
    acc = p_new * vn_ref[0]
    for j in range(n_pages):
        acc = acc + jnp.dot(p[:, j * page:(j + 1) * page].astype(BF16), v_pages[j][0].astype(BF16),
                            preferred_element_type=F32)
    lam = _lam_from(lams_ref[...], li)
    r1 = lax.broadcasted_iota(jnp.int32, (SCORE_ROWS, 1), 0)
    wgt = jnp.where(r1 % 2 == 0, 1.0, -lam) / l
    keep = (col // HEAD_W == row // 2) & (row < 2 * ATT_HEADS)
    d = jnp.sum(jnp.where(keep, acc * wgt, 0.0), axis=0, keepdims=True)
    c1 = lax.broadcasted_iota(jnp.int32, (1, ATT_WIDTH), 1)
    inv = jnp.zeros((1, ATT_WIDTH), F32)
    for h in range(ATT_HEADS):
        sel = c1 // HEAD_W == h
        ms = jnp.sum(jnp.where(sel, d * d, 0.0), axis=-1, keepdims=True) / HEAD_W
        inv = jnp.where(sel, lax.rsqrt(ms + EPS), inv)
    o_ref[0] = (d * inv * g_ref[...] * (1.0 - _lambda_init(li))).astype(BF16)


def _attn_sample(page_table, lams, aq, ak, av, g_subln4, cache_k, cache_v, li):
    n_seq, n_pages = page_table.shape
    depth, n_pool, page = cache_k.shape[:3]
    ck = cache_k.reshape(depth * n_pool, page, ATT_WIDTH)
    cv = cache_v.reshape(depth * n_pool, page, ATT_WIDTH)
    base = li * n_pool

    def page_spec(j):
        return pl.BlockSpec((1, page, ATT_WIDTH), lambda s, pt: (base + pt[s, j], 0, 0))

    tok = pl.BlockSpec((1, 1, ATT_WIDTH), lambda s, pt: (s, 0, 0))
    grid_spec = pltpu.PrefetchScalarGridSpec(
        num_scalar_prefetch=1,
        grid=(n_seq,),
        in_specs=[pl.BlockSpec(lams.shape, lambda s, pt: (0, 0)), tok, tok, tok,
                  pl.BlockSpec((1, ATT_WIDTH), lambda s, pt: (0, 0))]
        + [page_spec(j) for j in range(n_pages)] * 2,
        out_specs=tok,
    )
    out = pl.pallas_call(
        functools.partial(_attn_sample_kernel, li=li, n_pages=n_pages, page=page),
        grid_spec=grid_spec,
        out_shape=jax.ShapeDtypeStruct((n_seq, 1, ATT_WIDTH), BF16),
        compiler_params=pltpu.CompilerParams(dimension_semantics=("arbitrary",),
                                             vmem_limit_bytes=VMEM_LIMIT),
        name="attn_sample",
    )(page_table, lams, aq.reshape(n_seq, 1, ATT_WIDTH), ak.reshape(n_seq, 1, ATT_WIDTH),
      av.reshape(n_seq, 1, ATT_WIDTH), g_subln4, *([ck] * n_pages), *([cv] * n_pages))
    return out.reshape(n_seq, ATT_WIDTH)


def _split3(x):
    hi = x.astype(BF16)
    r = x - hi.astype(F32)
    mid = r.astype(BF16)
    lo = (r - mid.astype(F32)).astype(BF16)
    return hi, mid, lo


def _hgrn_prompt_kernel(tri_ref, q_ref, k_ref, v_ref, lf_ref, gate_ref, grec_ref,
                        o_ref, s_ref, st_ref, b_ref, acc_ref, *, tt):
    t = pl.program_id(1)
    nt = pl.num_programs(1)
    c = HGRN_CHUNK

    @pl.when(t == 0)
    def _():
        st_ref[...] = jnp.zeros_like(st_ref)

    tri = tri_ref[...]
    for r0 in range(0, tt, HGRN_CUM_BLOCK):
        lf = lf_ref[r0:r0 + HGRN_CUM_BLOCK, :]
        hi, mid, lo = _split3(lf)
        b_ref[r0:r0 + HGRN_CUM_BLOCK, :] = (
            jnp.dot(tri, hi, preferred_element_type=F32)
            + jnp.dot(tri, mid, preferred_element_type=F32)
            + jnp.dot(tri, lo, preferred_element_type=F32))

    rowi = lax.broadcasted_iota(jnp.int32, (c, c), 0)
    coli = lax.broadcasted_iota(jnp.int32, (c, c), 1)
    same_band = (rowi // BAND) == (coli // BAND)
    delta = rowi - coli

    def chunk_body(ci, carry):
        r0 = pl.multiple_of(ci * c, c)
        for h in range(REC_HEADS):
            cs = slice(h * REC_DK, (h + 1) * REC_DK)
            q = q_ref[pl.ds(r0, c), cs].astype(F32)
            k = k_ref[pl.ds(r0, c), cs].astype(F32)
            vb = v_ref[pl.ds(r0, c), cs]
            b = b_ref[pl.ds(r0, c), cs]
            st = st_ref[h]
            o = _nt_dot((q * jnp.exp(b)).astype(BF16), st.astype(BF16))
            bl = b[c - 1:c, :]
            kd = (k * jnp.exp(bl - b)).astype(BF16)
            st_ref[h] = st * jnp.exp(bl) + _tn_dot(vb, kd)
            a = jnp.zeros((c, c), F32)
            for half in HGRN_LEVELS:
                blk = 2 * half
                ref = jnp.concatenate(
                    [jnp.broadcast_to(b[g * blk + half - 1:g * blk + half, :], (blk, REC_DK))
                     for g in range(c // blk)], axis=0)
                qf = (q * jnp.exp(jnp.minimum(b - ref, 0.0))).astype(BF16)
                kf = (k * jnp.exp(jnp.minimum(ref - b, 0.0))).astype(BF16)
                mask = ((rowi // blk) == (coli // blk)) & ((rowi % blk) >= half) & ((coli % blk) < half)
                a = a + jnp.where(mask, _nt_dot(qf, kf), 0.0)
            for dist in range(BAND):
                if dist == 0:
                    prod = q * k
                else:
                    ks = pltpu.roll(k, dist, 0)
                    bs = pltpu.roll(b, dist, 0)
                    prod = q * ks * jnp.exp(jnp.minimum(b - bs, 0.0))
                w = jnp.sum(prod, axis=-1, keepdims=True)
                a = a + jnp.where((delta == dist) & same_band, w, 0.0)
            o = o + jnp.dot(a.astype(BF16), vb, preferred_element_type=F32)
            acc_ref[pl.ds(r0, c), cs] = o
        return carry

    lax.fori_loop(0, tt // c, chunk_body, 0)

    o_all = acc_ref[...]
    o_ref[...] = (_rms(o_all, grec_ref[...]) * gate_ref[...].astype(F32)).astype(BF16)

    @pl.when(t == nt - 1)
    def _():
        for h in range(REC_HEADS):
            s_ref[0, h] = st_ref[h].T


def _hgrn_tri():
    i = np.arange(HGRN_CUM_BLOCK)
    m = (i[:, None] >= i[None, :]) & ((i[:, None] // HGRN_CHUNK) == (i[None, :] // HGRN_CHUNK))
    return jnp.asarray(m.astype(np.float32), dtype=BF16)


def _hgrn_prompt(rq, rk, rv, lf, rg, g_rec, batch, seq, tt):
    nt = seq // tt
    n = batch * seq
    tile = pl.BlockSpec((tt, REC_WIDTH), lambda b, t: (b * nt + t, 0))
    return pl.pallas_call(
        functools.partial(_hgrn_prompt_kernel, tt=tt),
        grid=(batch, nt),
        in_specs=[pl.BlockSpec((HGRN_CUM_BLOCK, HGRN_CUM_BLOCK), lambda b, t: (0, 0)),
                  tile, tile, tile, tile, tile,
                  pl.BlockSpec((1, REC_WIDTH), lambda b, t: (0, 0))],
        out_specs=[tile, pl.BlockSpec((1, REC_HEADS, REC_DK, REC_DK), lambda b, t: (b, 0, 0, 0))],
        out_shape=[jax.ShapeDtypeStruct((n, REC_WIDTH), BF16),
                   jax.ShapeDtypeStruct((batch, REC_HEADS, REC_DK, REC_DK), F32)],
        scratch_shapes=[pltpu.VMEM((REC_HEADS, REC_DK, REC_DK), F32),
                        pltpu.VMEM((tt, REC_WIDTH), F32),
                        pltpu.VMEM((tt, REC_WIDTH), F32)],
        compiler_params=pltpu.CompilerParams(dimension_semantics=("arbitrary", "arbitrary"),
                                             vmem_limit_bytes=VMEM_LIMIT),
        name="hgrn_prompt",
    )(_hgrn_tri(), rq, rk, rv, lf, rg, g_rec)


def _hgrn_sample_kernel(q_ref, k_ref, v_ref, lf_ref, gate_ref, grec_ref, s0_ref, o_ref, s_ref, *, group):
    rowi = lax.broadcasted_iota(jnp.int32, (group, REC_DK), 0)
    outs = []
    for h in range(REC_HEADS):
        cs = slice(h * REC_DK, (h + 1) * REC_DK)
        q = q_ref[:, cs]
        k = k_ref[:, cs]
        v = v_ref[:, cs]
        f = jnp.exp(lf_ref[:, cs])
        o_h = jnp.zeros((group, REC_DK), F32)
        for n in range(group):
            st = s0_ref[n, h].T
            vn = jnp.where(rowi == n, v, jnp.zeros_like(v))
            st = st * f[n:n + 1, :] + _tn_dot(vn, k)
            s_ref[n, h] = st.T
            o_h = jnp.where(rowi == n, _nt_dot(q, st.astype(BF16)), o_h)
        outs.append(o_h)
    o = jnp.concatenate(outs, axis=1)
    o_ref[...] = (_rms(o, grec_ref[...]) * gate_ref[...].astype(F32)).astype(BF16)


def _hgrn_sample(rq, rk, rv, lf, rg, g_rec, s0, group):
    n = rq.shape[0]
    tile = pl.BlockSpec((group, REC_WIDTH), lambda i: (i, 0))
    st = pl.BlockSpec((group, REC_HEADS, REC_DK, REC_DK), lambda i: (i, 0, 0, 0))
    return pl.pallas_call(
        functools.partial(_hgrn_sample_kernel, group=group),
        grid=(n // group,),
        in_specs=[tile, tile, tile, tile, tile, pl.BlockSpec((1, REC_WIDTH), lambda i: (0, 0)), st],
        out_specs=[tile, st],
        out_shape=[jax.ShapeDtypeStruct((n, REC_WIDTH), BF16),
                   jax.ShapeDtypeStruct(s0.shape, F32)],
        compiler_params=pltpu.CompilerParams(dimension_semantics=("arbitrary",),
                                             vmem_limit_bytes=VMEM_LIMIT),
        name="hgrn_sample",
    )(rq, rk, rv, lf, rg, g_rec, s0)


FF_BLOCK = 1024


def _tail_kernel(x_ref, att_ref, rec_ref, p_ref, wo_ref, gffn_ref, wup_ref, wdn_ref,
                 wg_ref, wp_ref, gfin_ref, y_ref, *, final):
    mix = (jnp.dot(att_ref[...], wo_ref[:ATT_WIDTH, :], preferred_element_type=F32)
           + jnp.dot(rec_ref[...], wo_ref[ATT_WIDTH:, :], preferred_element_type=F32))
    x = x_ref[...] + mix
    h = _rms(x, gffn_ref[...]).astype(BF16)
    for c0 in range(0, D_FF, FF_BLOCK):
        up = jnp.dot(h, wup_ref[:, c0:c0 + FF_BLOCK], preferred_element_type=F32)
        act = jnp.square(jnp.maximum(up, 0.0)).astype(BF16)
        x = x + jnp.dot(act, wdn_ref[c0:c0 + FF_BLOCK, :], preferred_element_type=F32)
    gate = jax.nn.sigmoid(jnp.dot(x.astype(BF16), wg_ref[...], preferred_element_type=F32))
    emb = jnp.dot(p_ref[...].astype(BF16), wp_ref[...], preferred_element_type=F32)
    x = x + gate * emb
    y_ref[...] = _rms(x, gfin_ref[...]) if final else x


def _tail(x, att, rec, p, w_out, g_ffn, w_up, w_down, w_gate, w_proj, g_final, final, tm):
    n = x.shape[0]
    row = lambda i: (i, 0)
    const = lambda i: (0, 0)

    def resident(a):
        return pl.BlockSpec(a.shape, const, pipeline_mode=pl.Buffered(1))

    return pl.pallas_call(
        functools.partial(_tail_kernel, final=final),
        grid=(n // tm,),
        in_specs=[pl.BlockSpec((tm, D_MODEL), row),
                  pl.BlockSpec((tm, ATT_WIDTH), row),
                  pl.BlockSpec((tm, REC_WIDTH), row),
                  pl.BlockSpec((tm, PLE_DIM), row),
                  resident(w_out), resident(g_ffn), resident(w_up), resident(w_down),
                  resident(w_gate), resident(w_proj), resident(g_final)],
        out_specs=pl.BlockSpec((tm, D_MODEL), row),
        out_shape=jax.ShapeDtypeStruct((n, D_MODEL), F32),
        compiler_params=pltpu.CompilerParams(dimension_semantics=("arbitrary",),
                                             vmem_limit_bytes=VMEM_LIMIT),
        name="tail",
    )(x, att, rec, p, w_out, g_ffn, w_up, w_down, w_gate, w_proj, g_final)


def kernel(x_prompt, x_sample, cache_k, cache_v, state_hgrn, page_table, p_prompt, p_sample,
           w_in, lambda_q1, lambda_k1, lambda_q2, lambda_k2, g_subln, hgrn_lb, g_rec, w_out,
           g_mix, g_ffn, w_up, w_down, w_ple_gate, w_ple_proj, g_final):
    batch, seq, d = x_prompt.shape
    n_seq = x_sample.shape[0]
    depth = w_in.shape[0]
    assert d == D_MODEL and w_in.shape[2] == N_PROJ * ATT_WIDTH and x_sample.shape[1] == 1
    assert cache_k.shape[3:] == (ATT_HEADS, HEAD_W) and state_hgrn.shape[2:] == (REC_HEADS, REC_DK, REC_DK)
    n_p = batch * seq

    hp = x_prompt.reshape(n_p, d)
    hs = x_sample.reshape(n_seq, d)
    row2 = lambda a: a.reshape(1, -1)
    outs = [[] for _ in range(6)]
    for li in range(depth):
        last = li == depth - 1
        w_in_b = w_in[li].astype(BF16)
        w_out_b = w_out[li].astype(BF16)
        w_up_b = w_up[li].astype(BF16)
        w_dn_b = w_down[li].astype(BF16)
        w_g_b = w_ple_gate[li].astype(BF16)
        w_p_b = w_ple_proj[li].astype(BF16)
        lams = jnp.stack([lambda_q1[li], lambda_k1[li], lambda_q2[li], lambda_k2[li]], axis=0)
        g_sub = row2(g_subln[li])
        g_sub4 = jnp.tile(g_sub, (1, ATT_HEADS))
        g_r = row2(g_rec[li])
        tail_w = (w_out_b, row2(g_ffn[li]), w_up_b, w_dn_b, w_g_b, w_p_b, row2(g_final))

        aq, ak, av, akb, avb, rq, lf, rk, rv, rg = _inproj(hp, row2(g_mix[li]), w_in_b, hgrn_lb, li, tm=512)
        att = _attn_prompt(lams, aq, akb, avb, g_sub, li, batch, seq, tq=256)
        rec, s_p = _hgrn_prompt(rq, rk, rv, lf, rg, g_r, batch, seq, tt=512)
        hp = _tail(hp, att, rec, p_prompt[li].reshape(n_p, PLE_DIM), *tail_w, final=last, tm=512)
        outs[0].append(ak.reshape(batch, seq, ATT_HEADS, HEAD_W))
        outs[1].append(av.reshape(batch, seq, ATT_HEADS, HEAD_W))
        outs[2].append(s_p)

        aq, ak, av, _, _, rq, lf, rk, rv, rg = _inproj(hs, row2(g_mix[li]), w_in_b, hgrn_lb, li, tm=n_seq)
        att = _attn_sample(page_table, lams, aq, ak, av, g_sub4, cache_k, cache_v, li)
        rec, s_s = _hgrn_sample(rq, rk, rv, lf, rg, g_r, state_hgrn[li], group=16)
        hs = _tail(hs, att, rec, p_sample[li].reshape(n_seq, PLE_DIM), *tail_w, final=last, tm=n_seq)
        outs[3].append(ak.reshape(n_seq, 1, ATT_HEADS, HEAD_W))
        outs[4].append(av.reshape(n_seq, 1, ATT_HEADS, HEAD_W))
        outs[5].append(s_s)

    k_p, v_p, s_p, k_s, v_s, s_s = [jnp.stack(o, axis=0) for o in outs]
    return (hp.reshape(batch, seq, d), hs.reshape(n_seq, 1, d), k_p, v_p, s_p, k_s, v_s, s_s)
```

```python
import functools
import math

import jax
import jax.numpy as jnp
import numpy as np
from jax import lax
from jax.experimental import pallas as pl
from jax.experimental.pallas import tpu as pltpu

F32 = jnp.float32
BF16 = jnp.bfloat16

D_MODEL = 1024
ATT_HEADS = 4
ATT_HD = 64
HEAD_W = 2 * ATT_HD
ATT_WIDTH = ATT_HEADS * HEAD_W
REC_HEADS = 4
REC_DK = 128
REC_WIDTH = REC_HEADS * REC_DK
N_PROJ = 7
PLE_DIM = 256
D_FF = 4 * D_MODEL
EPS = 1e-6
ATT_SCALE = ATT_HD ** -0.5 * math.log2(math.e)

V7X_VMEM_BYTES = 64 * 1024 * 1024
VMEM_LIMIT = 52 * 1024 * 1024

HGRN_CHUNK = 64
HGRN_CUM_BLOCK = 256
HGRN_LEVELS = (8, 16, 32)
BAND = 8


def _lambda_init(li):
    return 0.8 - 0.6 * math.exp(-0.3 * li)


def _rms(x, g):
    return x * lax.rsqrt(jnp.mean(x * x, axis=-1, keepdims=True) + EPS) * g


def _silu(x):
    return x * jax.nn.sigmoid(x)


def _nt_dot(a, b):
    return lax.dot_general(a, b, (((1,), (1,)), ((), ())), preferred_element_type=F32)


def _tn_dot(a, b):
    return lax.dot_general(a, b, (((0,), (0,)), ((), ())), preferred_element_type=F32)


def _lam_from(lams, li):
    a = jnp.sum(lams[0:1, :] * lams[1:2, :], axis=-1, keepdims=True)
    b = jnp.sum(lams[2:3, :] * lams[3:4, :], axis=-1, keepdims=True)
    return jnp.exp(a) - jnp.exp(b) + _lambda_init(li)


def _inproj_kernel(x_ref, g_ref, w_ref, lb_ref,
                   aq_ref, ak_ref, av_ref, akb_ref, avb_ref,
                   rq_ref, lf_ref, rk_ref, rv_ref, rg_ref, *, li):
    h = _rms(x_ref[...], g_ref[...]).astype(BF16)

    def proj(i):
        return jnp.dot(h, w_ref[:, i * ATT_WIDTH:(i + 1) * ATT_WIDTH], preferred_element_type=F32)

    aq_ref[...] = (proj(0) * ATT_SCALE).astype(BF16)
    for full_ref, half_ref, val in ((ak_ref, akb_ref, proj(1)), (av_ref, avb_ref, proj(2))):
        half_ref[...] = val.astype(BF16)
        for hd in range(ATT_HEADS):
            full_ref[:, hd, :] = val[:, hd * HEAD_W:(hd + 1) * HEAD_W]
    rq_ref[...] = _silu(proj(3)).astype(BF16)
    lbp = lb_ref[...]
    e = jnp.exp(lbp - jnp.max(lbp, axis=0, keepdims=True))
    lb = jnp.sum(e[:li + 1], axis=0, keepdims=True) / jnp.sum(e, axis=0, keepdims=True)
    f = lb + (1.0 - lb) * jax.nn.sigmoid(proj(4))
    lf_ref[...] = jnp.log(f)
    rk_ref[...] = (1.0 - f).astype(BF16)
    rv_ref[...] = proj(5).astype(BF16)
    rg_ref[...] = _silu(proj(6)).astype(BF16)


def _inproj(x, g_mix, w_in, hgrn_lb, li, tm):
    n = x.shape[0]
    grid = (n // tm,)
    row = lambda i: (i, 0)
    const = lambda i: (0, 0)
    wide = pl.BlockSpec((tm, ATT_WIDTH), row)
    heads = pl.BlockSpec((tm, ATT_HEADS, HEAD_W), lambda i: (i, 0, 0))
    flat = lambda dt: jax.ShapeDtypeStruct((n, ATT_WIDTH), dt)
    by_head = jax.ShapeDtypeStruct((n, ATT_HEADS, HEAD_W), F32)
    return pl.pallas_call(
        functools.partial(_inproj_kernel, li=li),
        grid=grid,
        in_specs=[pl.BlockSpec((tm, D_MODEL), row),
                  pl.BlockSpec((1, D_MODEL), const),
                  pl.BlockSpec(w_in.shape, const),
                  pl.BlockSpec(hgrn_lb.shape, const)],
        out_specs=[wide, heads, heads] + [wide] * 7,
        out_shape=[flat(BF16), by_head, by_head, flat(BF16), flat(BF16),
                   flat(BF16), flat(F32), flat(BF16), flat(BF16), flat(BF16)],
        compiler_params=pltpu.CompilerParams(dimension_semantics=("arbitrary",),
                                             vmem_limit_bytes=VMEM_LIMIT),
        name="inproj",
    )(x, g_mix, w_in, hgrn_lb)


def _attn_prompt_kernel(lams_ref, q_ref, k_ref, v_ref, g_ref, o_ref,
                        s_ref, m_ref, l_ref, acc_ref, *, li, tq):
    qi = pl.program_id(2)
    q = q_ref[...]
    lane = lax.broadcasted_iota(jnp.int32, q.shape, 1)
    zero = jnp.zeros_like(q)
    qs = jnp.concatenate([jnp.where(lane < ATT_HD, q, zero),
                          jnp.where(lane >= ATT_HD, q, zero)], axis=0)
    lane_blocks = tq // HEAD_W

    def fold(x, op):
        out = x[:, :HEAD_W]
        for b in range(1, lane_blocks):
            out = op(out, x[:, b * HEAD_W:(b + 1) * HEAD_W])
        return out

    def scores(j, diagonal):
        kb = k_ref[pl.ds(pl.multiple_of(j * tq, tq), tq), :]
        s = _nt_dot(qs, kb)
        if diagonal:
            r = lax.broadcasted_iota(jnp.int32, s.shape, 0)
            c = lax.broadcasted_iota(jnp.int32, s.shape, 1)
            r = jnp.where(r >= tq, r - tq, r)
            s = jnp.where(c <= r, s, -jnp.inf)
        s_ref[j] = s
        return fold(s, jnp.maximum)

    m_ref[...] = scores(qi, True)

    def pass1(j, carry):
        m_ref[...] = jnp.maximum(m_ref[...], scores(j, False))
        return carry

    lax.fori_loop(0, qi, pass1, 0)
    m_ref[...] = jnp.broadcast_to(jnp.max(m_ref[...], axis=-1, keepdims=True), m_ref.shape)
    l_ref[...] = jnp.zeros_like(l_ref)
    acc_ref[...] = jnp.zeros_like(acc_ref)

    def pass2(j, carry):
        s = s_ref[j]
        mb = m_ref[...]
        p = jnp.concatenate([jnp.exp2(s[:, b * HEAD_W:(b + 1) * HEAD_W] - mb)
                             for b in range(lane_blocks)], axis=1)
        l_ref[...] += fold(p, jnp.add)
        vb = v_ref[pl.ds(pl.multiple_of(j * tq, tq), tq), :]
        acc_ref[...] += jnp.dot(p.astype(BF16), vb, preferred_element_type=F32)
        return carry

    lax.fori_loop(0, qi + 1, pass2, 0)
    o = acc_ref[...] / jnp.sum(l_ref[...], axis=-1, keepdims=True)
    lam = _lam_from(lams_ref[...], li)
    d = o[:tq] - lam * o[tq:]
    o_ref[...] = (_rms(d, g_ref[...]) * (1.0 - _lambda_init(li))).astype(BF16)


def _attn_prompt(lams, aq, akb, avb, g_subln, li, batch, seq, tq):
    nq = seq // tq
    n = batch * seq
    return pl.pallas_call(
        functools.partial(_attn_prompt_kernel, li=li, tq=tq),
        grid=(batch, ATT_HEADS, nq),
        in_specs=[pl.BlockSpec(lams.shape, lambda b, h, i: (0, 0)),
                  pl.BlockSpec((tq, HEAD_W), lambda b, h, i: (b * nq + i, h)),
                  pl.BlockSpec((seq, HEAD_W), lambda b, h, i: (b, h)),
                  pl.BlockSpec((seq, HEAD_W), lambda b, h, i: (b, h)),
                  pl.BlockSpec((1, HEAD_W), lambda b, h, i: (0, 0))],
        out_specs=pl.BlockSpec((tq, HEAD_W), lambda b, h, i: (b * nq + i, h)),
        out_shape=jax.ShapeDtypeStruct((n, ATT_WIDTH), BF16),
        scratch_shapes=[pltpu.VMEM((nq, 2 * tq, tq), F32),
                        pltpu.VMEM((2 * tq, HEAD_W), F32),
                        pltpu.VMEM((2 * tq, HEAD_W), F32),
                        pltpu.VMEM((2 * tq, HEAD_W), F32)],
        compiler_params=pltpu.CompilerParams(
            dimension_semantics=("arbitrary", "arbitrary", "arbitrary"),
            vmem_limit_bytes=VMEM_LIMIT),
        name="attn_prompt",
    )(lams, aq, akb, avb, g_subln)


SCORE_ROWS = 16


def _attn_sample_kernel(pt_ref, lams_ref, q_ref, kn_ref, vn_ref, g_ref, *rest, li, n_pages):
    del pt_ref
    k_pages = rest[:n_pages]
    v_pages = rest[n_pages:2 * n_pages]
    o_ref = rest[2 * n_pages]
    pr = k_pages[0].shape[1]
    q = q_ref[0].astype(F32)
    kn = kn_ref[0]
    vn = vn_ref[0]
    row = lax.broadcasted_iota(jnp.int32, (SCORE_ROWS, HEAD_W), 0)
    lane = lax.broadcasted_iota(jnp.int32, (SCORE_ROWS, HEAD_W), 1)
    qrows = jnp.zeros((SCORE_ROWS, HEAD_W), F32)
    knrows = jnp.zeros((SCORE_ROWS, HEAD_W), F32)
    vnrows = jnp.zeros((SCORE_ROWS, HEAD_W), F32)
    for h in range(ATT_HEADS):
        cs = slice(h * HEAD_W, (h + 1) * HEAD_W)
        in_head = row // 2 == h
        qrows = jnp.where(in_head & (lane // ATT_HD == row % 2), q[:, cs], qrows)
        knrows = jnp.where(in_head, kn[:, cs], knrows)
        vnrows = jnp.where(in_head, vn[:, cs], vnrows)
    qb = qrows.astype(BF16)
    s_past = jnp.concatenate([_nt_dot(qb, kp[0].astype(BF16)) for kp in k_pages], axis=1)
    srow = lax.broadcasted_iota(jnp.int32, s_past.shape, 0)
    scol = lax.broadcasted_iota(jnp.int32, s_past.shape, 1)
    own = (scol % ATT_HEADS) == (srow // 2)
    s_past = jnp.where(own, s_past, -jnp.inf)
    s_new = jnp.sum(qrows * knrows, axis=-1, keepdims=True)
    m = jnp.maximum(jnp.max(s_past, axis=-1, keepdims=True), s_new)
    p = jnp.exp2(s_past - m)
    p_new = jnp.exp2(s_new - m)
    l = jnp.sum(p, axis=-1, keepdims=True) + p_new
    acc = p_new * vnrows
    for j in range(n_pages):
        acc = acc + jnp.dot(p[:, j * pr:(j + 1) * pr].astype(BF16), v_pages[j][0].astype(BF16),
                            preferred_element_type=F32)
    o = acc / l
    lam = _lam_from(lams_ref[...], li)
    outs = []
    for h in range(ATT_HEADS):
        d = o[2 * h:2 * h + 1, :] - lam * o[2 * h + 1:2 * h + 2, :]
        outs.append(_rms(d, g_ref[...]) * (1.0 - _lambda_init(li)))
    o_ref[0] = jnp.concatenate(outs, axis=1).astype(BF16)


def _attn_sample(page_table, lams, aq, ak, av, g_subln, cache_k, cache_v, li):
    n_seq, n_pages = page_table.shape
    depth, n_pool, page = cache_k.shape[:3]
    pr = page * ATT_HEADS
    ck = cache_k.reshape(depth * n_pool, pr, HEAD_W)
    cv = cache_v.reshape(depth * n_pool, pr, HEAD_W)
    base = li * n_pool

    def page_spec(j):
        return pl.BlockSpec((1, pr, HEAD_W), lambda s, pt: (base + pt[s, j], 0, 0))

    tok = pl.BlockSpec((1, 1, ATT_WIDTH), lambda s, pt: (s, 0, 0))
    grid_spec = pltpu.PrefetchScalarGridSpec(
        num_scalar_prefetch=1,
        grid=(n_seq,),
        in_specs=[pl.BlockSpec(lams.shape, lambda s, pt: (0, 0)), tok, tok, tok,
                  pl.BlockSpec((1, HEAD_W), lambda s, pt: (0, 0))]
        + [page_spec(j) for j in range(n_pages)] * 2,
        out_specs=tok,
    )
    out = pl.pallas_call(
        functools.partial(_attn_sample_kernel, li=li, n_pages=n_pages),
        grid_spec=grid_spec,
        out_shape=jax.ShapeDtypeStruct((n_seq, 1, ATT_WIDTH), BF16),
        compiler_params=pltpu.CompilerParams(dimension_semantics=("arbitrary",),
                                             vmem_limit_bytes=VMEM_LIMIT),
        name="attn_sample",
    )(page_table, lams, aq.reshape(n_seq, 1, ATT_WIDTH), ak.reshape(n_seq, 1, ATT_WIDTH),
      av.reshape(n_seq, 1, ATT_WIDTH), g_subln, *([ck] * n_pages), *([cv] * n_pages))
    return out.reshape(n_seq, ATT_WIDTH)


def _split3(x):
    hi = x.astype(BF16)
    r = x - hi.astype(F32)
    mid = r.astype(BF16)
    lo = (r - mid.astype(F32)).astype(BF16)
    return hi, mid, lo


def _hgrn_prompt_kernel(tri_ref, q_ref, k_ref, v_ref, lf_ref, gate_ref, grec_ref,
                        o_ref, s_ref, st_ref, b_ref, acc_ref, *, tt):
    t = pl.program_id(1)
    nt = pl.num_programs(1)
    c = HGRN_CHUNK

    @pl.when(t == 0)
    def _():
        st_ref[...] = jnp.zeros_like(st_ref)

    tri = tri_ref[...]
    for r0 in range(0, tt, HGRN_CUM_BLOCK):
        lf = lf_ref[r0:r0 + HGRN_CUM_BLOCK, :]
        hi, mid, lo = _split3(lf)
        b_ref[r0:r0 + HGRN_CUM_BLOCK, :] = (
            jnp.dot(tri, hi, preferred_element_type=F32)
            + jnp.dot(tri, mid, preferred_element_type=F32)
            + jnp.dot(tri, lo, preferred_element_type=F32))

    rowi = lax.broadcasted_iota(jnp.int32, (c, c), 0)
    coli = lax.broadcasted_iota(jnp.int32, (c, c), 1)
    same_band = (rowi // BAND) == (coli // BAND)
    delta = rowi - coli

    def chunk_body(ci, carry):
        r0 = pl.multiple_of(ci * c, c)
        for h in range(REC_HEADS):
            cs = slice(h * REC_DK, (h + 1) * REC_DK)
            q = q_ref[pl.ds(r0, c), cs].astype(F32)
            k = k_ref[pl.ds(r0, c), cs].astype(F32)
            vb = v_ref[pl.ds(r0, c), cs]
            b = b_ref[pl.ds(r0, c), cs]
            st = st_ref[h]
            o = _nt_dot((q * jnp.exp(b)).astype(BF16), st.astype(BF16))
            bl = b[c - 1:c, :]
            kd = (k * jnp.exp(bl - b)).astype(BF16)
            st_ref[h] = st * jnp.exp(bl) + _tn_dot(vb, kd)
            a = jnp.zeros((c, c), F32)
            for half in HGRN_LEVELS:
                blk = 2 * half
                ref = jnp.concatenate(
                    [jnp.broadcast_to(b[g * blk + half - 1:g * blk + half, :], (blk, REC_DK))
                     for g in range(c // blk)], axis=0)
                qf = (q * jnp.exp(jnp.minimum(b - ref, 0.0))).astype(BF16)
                kf = (k * jnp.exp(jnp.minimum(ref - b, 0.0))).astype(BF16)
                mask = ((rowi // blk) == (coli // blk)) & ((rowi % blk) >= half) & ((coli % blk) < half)
                a = a + jnp.where(mask, _nt_dot(qf, kf), 0.0)
            for dist in range(BAND):
                if dist == 0:
                    prod = q * k
                else:
                    ks = pltpu.roll(k, dist, 0)
                    bs = pltpu.roll(b, dist, 0)
                    prod = q * ks * jnp.exp(jnp.minimum(b - bs, 0.0))
                w = jnp.sum(prod, axis=-1, keepdims=True)
                a = a + jnp.where((delta == dist) & same_band, w, 0.0)
            o = o + jnp.dot(a.astype(BF16), vb, preferred_element_type=F32)
            acc_ref[pl.ds(r0, c), cs] = o
        return carry

    lax.fori_loop(0, tt // c, chunk_body, 0)

    o_all = acc_ref[...]
    o_ref[...] = (_rms(o_all, grec_ref[...]) * gate_ref[...].astype(F32)).astype(BF16)

    @pl.when(t == nt - 1)
    def _():
        for h in range(REC_HEADS):
            s_ref[0, h] = st_ref[h].T


def _hgrn_tri():
    i = np.arange(HGRN_CUM_BLOCK)
    m = (i[:, None] >= i[None, :]) & ((i[:, None] // HGRN_CHUNK) == (i[None, :] // HGRN_CHUNK))
    return jnp.asarray(m.astype(np.float32), dtype=BF16)


def _hgrn_prompt(rq, rk, rv, lf, rg, g_rec, batch, seq, tt):
    nt = seq // tt
    n = batch * seq
    tile = pl.BlockSpec((tt, REC_WIDTH), lambda b, t: (b * nt + t, 0))
    return pl.pallas_call(
        functools.partial(_hgrn_prompt_kernel, tt=tt),
        grid=(batch, nt),
        in_specs=[pl.BlockSpec((HGRN_CUM_BLOCK, HGRN_CUM_BLOCK), lambda b, t: (0, 0)),
                  tile, tile, tile, tile, tile,
                  pl.BlockSpec((1, REC_WIDTH), lambda b, t: (0, 0))],
        out_specs=[tile, pl.BlockSpec((1, REC_HEADS, REC_DK, REC_DK), lambda b, t: (b, 0, 0, 0))],
        out_shape=[jax.ShapeDtypeStruct((n, REC_WIDTH), BF16),
                   jax.ShapeDtypeStruct((batch, REC_HEADS, REC_DK, REC_DK), F32)],
        scratch_shapes=[pltpu.VMEM((REC_HEADS, REC_DK, REC_DK), F32),
                        pltpu.VMEM((tt, REC_WIDTH), F32),
                        pltpu.VMEM((tt, REC_WIDTH), F32)],
        compiler_params=pltpu.CompilerParams(dimension_semantics=("arbitrary", "arbitrary"),
                                             vmem_limit_bytes=VMEM_LIMIT),
        name="hgrn_prompt",
    )(_hgrn_tri(), rq, rk, rv, lf, rg, g_rec)


def _hgrn_sample_kernel(q_ref, k_ref, v_ref, lf_ref, gate_ref, grec_ref, s0_ref, o_ref, s_ref, *, group):
    rowi = lax.broadcasted_iota(jnp.int32, (group, REC_DK), 0)
    outs = []
    for h in range(REC_HEADS):
        cs = slice(h * REC_DK, (h + 1) * REC_DK)
        q = q_ref[:, cs]
        k = k_ref[:, cs]
        v = v_ref[:, cs]
        f = jnp.exp(lf_ref[:, cs])
        o_h = jnp.zeros((group, REC_DK), F32)
        for n in range(group):
            st = s0_ref[n, h].T
            vn = jnp.where(rowi == n, v, jnp.zeros_like(v))
            st = st * f[n:n + 1, :] + _tn_dot(vn, k)
            s_ref[n, h] = st.T
            o_h = jnp.where(rowi == n, _nt_dot(q, st.astype(BF16)), o_h)
        outs.append(o_h)
    o = jnp.concatenate(outs, axis=1)
    o_ref[...] = (_rms(o, grec_ref[...]) * gate_ref[...].astype(F32)).astype(BF16)


def _hgrn_sample(rq, rk, rv, lf, rg, g_rec, s0, group):
    n = rq.shape[0]
    tile = pl.BlockSpec((group, REC_WIDTH), lambda i: (i, 0))
    st = pl.BlockSpec((group, REC_HEADS, REC_DK, REC_DK), lambda i: (i, 0, 0, 0))
    return pl.pallas_call(
        functools.partial(_hgrn_sample_kernel, group=group),
        grid=(n // group,),
        in_specs=[tile, tile, tile, tile, tile, pl.BlockSpec((1, REC_WIDTH), lambda i: (0, 0)), st],
        out_specs=[tile, st],
        out_shape=[jax.ShapeDtypeStruct((n, REC_WIDTH), BF16),
                   jax.ShapeDtypeStruct(s0.shape, F32)],
        compiler_params=pltpu.CompilerParams(dimension_semantics=("arbitrary",),
                                             vmem_limit_bytes=VMEM_LIMIT),
        name="hgrn_sample",
    )(rq, rk, rv, lf, rg, g_rec, s0)


FF_BLOCK = 1024


def _tail_kernel(x_ref, att_ref, rec_ref, p_ref, wo_ref, gffn_ref, wup_ref, wdn_ref,
                 wg_ref, wp_ref, gfin_ref, y_ref, *, final):
    mix = (jnp.dot(att_ref[...], wo_ref[:ATT_WIDTH, :], preferred_element_type=F32)
           + jnp.dot(rec_ref[...], wo_ref[ATT_WIDTH:, :], preferred_element_type=F32))
    x = x_ref[...] + mix
    h = _rms(x, gffn_ref[...]).astype(BF16)
    for c0 in range(0, D_FF, FF_BLOCK):
        up = jnp.dot(h, wup_ref[:, c0:c0 + FF_BLOCK], preferred_element_type=F32)
        act = jnp.square(jnp.maximum(up, 0.0)).astype(BF16)
        x = x + jnp.dot(act, wdn_ref[c0:c0 + FF_BLOCK, :], preferred_element_type=F32)
    gate = jax.nn.sigmoid(jnp.dot(x.astype(BF16), wg_ref[...], preferred_element_type=F32))
    emb = jnp.dot(p_ref[...].astype(BF16), wp_ref[...], preferred_element_type=F32)
    x = x + gate * emb
    y_ref[...] = _rms(x, gfin_ref[...]) if final else x


def _tail(x, att, rec, p, w_out, g_ffn, w_up, w_down, w_gate, w_proj, g_final, final, tm):
    n = x.shape[0]
    row = lambda i: (i, 0)
    const = lambda i: (0, 0)

    def resident(a):
        return pl.BlockSpec(a.shape, const, pipeline_mode=pl.Buffered(1))

    return pl.pallas_call(
        functools.partial(_tail_kernel, final=final),
        grid=(n // tm,),
        in_specs=[pl.BlockSpec((tm, D_MODEL), row),
                  pl.BlockSpec((tm, ATT_WIDTH), row),
                  pl.BlockSpec((tm, REC_WIDTH), row),
                  pl.BlockSpec((tm, PLE_DIM), row),
                  resident(w_out), resident(g_ffn), resident(w_up), resident(w_down),
                  resident(w_gate), resident(w_proj), resident(g_final)],
        out_specs=pl.BlockSpec((tm, D_MODEL), row),
        out_shape=jax.ShapeDtypeStruct((n, D_MODEL), F32),
        compiler_params=pltpu.CompilerParams(dimension_semantics=("arbitrary",),
                                             vmem_limit_bytes=VMEM_LIMIT),
        name="tail",
    )(x, att, rec, p, w_out, g_ffn, w_up, w_down, w_gate, w_proj, g_final)


def kernel(x_prompt, x_sample, cache_k, cache_v, state_hgrn, page_table, p_prompt, p_sample,
           w_in, lambda_q1, lambda_k1, lambda_q2, lambda_k2, g_subln, hgrn_lb, g_rec, w_out,
           g_mix, g_ffn, w_up, w_down, w_ple_gate, w_ple_proj, g_final):
    batch, seq, d = x_prompt.shape
    n_seq = x_sample.shape[0]
    depth = w_in.shape[0]
    assert d == D_MODEL and w_in.shape[2] == N_PROJ * ATT_WIDTH and x_sample.shape[1] == 1
    assert cache_k.shape[3:] == (ATT_HEADS, HEAD_W) and state_hgrn.shape[2:] == (REC_HEADS, REC_DK, REC_DK)
    n_p = batch * seq

    hp = x_prompt.reshape(n_p, d)
    hs = x_sample.reshape(n_seq, d)
    row2 = lambda a: a.reshape(1, -1)
    outs = [[] for _ in range(6)]
    for li in range(depth):
        last = li == depth - 1
        w_in_b = w_in[li].astype(BF16)
        w_out_b = w_out[li].astype(BF16)
        w_up_b = w_up[li].astype(BF16)
        w_dn_b = w_down[li].astype(BF16)
        w_g_b = w_ple_gate[li].astype(BF16)
        w_p_b = w_ple_proj[li].astype(BF16)
        lams = jnp.stack([lambda_q1[li], lambda_k1[li], lambda_q2[li], lambda_k2[li]], axis=0)
        g_sub = row2(g_subln[li])
        g_r = row2(g_rec[li])
        tail_w = (w_out_b, row2(g_ffn[li]), w_up_b, w_dn_b, w_g_b, w_p_b, row2(g_final))

        aq, ak, av, akb, avb, rq, lf, rk, rv, rg = _inproj(hp, row2(g_mix[li]), w_in_b, hgrn_lb, li, tm=512)
        att = _attn_prompt(lams, aq, akb, avb, g_sub, li, batch, seq, tq=512)
        rec, s_p = _hgrn_prompt(rq, rk, rv, lf, rg, g_r, batch, seq, tt=512)
        hp = _tail(hp, att, rec, p_prompt[li].reshape(n_p, PLE_DIM), *tail_w, final=last, tm=512)
        outs[0].append(ak.reshape(batch, seq, ATT_HEADS, HEAD_W))
        outs[1].append(av.reshape(batch, seq, ATT_HEADS, HEAD_W))
        outs[2].append(s_p)

        aq, ak, av, _, _, rq, lf, rk, rv, rg = _inproj(hs, row2(g_mix[li]), w_in_b, hgrn_lb, li, tm=n_seq)
        att = _attn_sample(page_table, lams, aq, ak, av, g_sub, cache_k, cache_v, li)
        rec, s_s = _hgrn_sample(rq, rk, rv, lf, rg, g_r, state_hgrn[li], group=16)
        hs = _tail(hs, att, rec, p_sample[li].reshape(n_seq, PLE_DIM), *tail_w, final=last, tm=n_seq)
        outs[3].append(ak.reshape(n_seq, 1, ATT_HEADS, HEAD_W))
        outs[4].append(av.reshape(n_seq, 1, ATT_HEADS, HEAD_W))
        outs[5].append(s_s)

    k_p, v_p, s_p, k_s, v_s, s_s = [jnp.stack(o, axis=0) for o in outs]
    return (hp.reshape(batch, seq, d), hs.reshape(n_seq, 1, d), k_p, v_p, s_p, k_s, v_s, s_s)
```

```python
import functools
import math

import jax
import jax.numpy as jnp
import numpy as np
from jax import lax
from jax.experimental import pallas as pl
from jax.experimental.pallas import tpu as pltpu

F32 = jnp.float32
BF16 = jnp.bfloat16

D_MODEL = 1024
ATT_HEADS = 4
ATT_HD = 64
HEAD_W = 2 * ATT_HD
ATT_WIDTH = ATT_HEADS * HEAD_W
REC_HEADS = 4
REC_DK = 128
REC_WIDTH = REC_HEADS * REC_DK
N_PROJ = 7
PLE_DIM = 256
D_FF = 4 * D_MODEL
EPS = 1e-6
LOG2E = math.log2(math.e)
ATT_SCALE = ATT_HD ** -0.5 * LOG2E

V7X_VMEM_BYTES = 64 * 1024 * 1024
VMEM_LIMIT = 52 * 1024 * 1024

HGRN_CHUNK = 64
HGRN_CUM_BLOCK = 256
BAND = 4
HGRN_LEVELS = (4, 8, 16, 32)
SUBLANES = 8


def _lambda_init(li):
    return 0.8 - 0.6 * math.exp(-0.3 * li)


def _rms(x, g):
    return x * lax.rsqrt(jnp.mean(x * x, axis=-1, keepdims=True) + EPS) * g


def _silu(x):
    return x * jax.nn.sigmoid(x)


def _nt_dot(a, b):
    return lax.dot_general(a, b, (((1,), (1,)), ((), ())), preferred_element_type=F32)


def _tn_dot(a, b):
    return lax.dot_general(a, b, (((0,), (0,)), ((), ())), preferred_element_type=F32)


def _lam_from(lams, li):
    a = jnp.sum(lams[0:1, :] * lams[1:2, :], axis=-1, keepdims=True)
    b = jnp.sum(lams[2:3, :] * lams[3:4, :], axis=-1, keepdims=True)
    return jnp.exp(a) - jnp.exp(b) + _lambda_init(li)


def _inproj_kernel(x_ref, g_ref, w_ref, lb_ref,
                   aq_ref, ak_ref, av_ref, akb_ref, avb_ref,
                   rq_ref, lf_ref, rk_ref, rv_ref, rg_ref, *, li):
    h = _rms(x_ref[...], g_ref[...]).astype(BF16)

    def proj(i):
        return jnp.dot(h, w_ref[:, i * ATT_WIDTH:(i + 1) * ATT_WIDTH], preferred_element_type=F32)

    aq_ref[...] = (proj(0) * ATT_SCALE).astype(BF16)
    tm = x_ref.shape[0]
    for full_ref, half_ref, val in ((ak_ref, akb_ref, proj(1)), (av_ref, avb_ref, proj(2))):
        half_ref[...] = val.astype(BF16)
        for hd in range(ATT_HEADS):
            full_ref[pl.ds(hd, tm, stride=ATT_HEADS), :] = val[:, hd * HEAD_W:(hd + 1) * HEAD_W]
    rq_ref[...] = _silu(proj(3)).astype(BF16)
    lbp = lb_ref[...]
    e = jnp.exp(lbp - jnp.max(lbp, axis=0, keepdims=True))
    lb = jnp.sum(e[:li + 1], axis=0, keepdims=True) / jnp.sum(e, axis=0, keepdims=True)
    f = lb + (1.0 - lb) * jax.nn.sigmoid(proj(4))
    lf_ref[...] = jnp.log(f)
    rk_ref[...] = (1.0 - f).astype(BF16)
    rv_ref[...] = proj(5).astype(BF16)
    rg_ref[...] = _silu(proj(6)).astype(BF16)


def _inproj(x, g_mix, w_in, hgrn_lb, li, tm):
    n = x.shape[0]
    grid = (n // tm,)
    row = lambda i: (i, 0)
    const = lambda i: (0, 0)
    wide = pl.BlockSpec((tm, ATT_WIDTH), row)
    heads = pl.BlockSpec((tm * ATT_HEADS, HEAD_W), row)
    flat = lambda dt: jax.ShapeDtypeStruct((n, ATT_WIDTH), dt)
    by_head = jax.ShapeDtypeStruct((n * ATT_HEADS, HEAD_W), F32)
    return pl.pallas_call(
        functools.partial(_inproj_kernel, li=li),
        grid=grid,
        in_specs=[pl.BlockSpec((tm, D_MODEL), row),
                  pl.BlockSpec((1, D_MODEL), const),
                  pl.BlockSpec(w_in.shape, const),
                  pl.BlockSpec(hgrn_lb.shape, const)],
        out_specs=[wide, heads, heads] + [wide] * 7,
        out_shape=[flat(BF16), by_head, by_head, flat(BF16), flat(BF16),
                   flat(BF16), flat(F32), flat(BF16), flat(BF16), flat(BF16)],
        compiler_params=pltpu.CompilerParams(dimension_semantics=("arbitrary",),
                                             vmem_limit_bytes=VMEM_LIMIT),
        name="inproj",
    )(x, g_mix, w_in, hgrn_lb)


def _attn_prompt_kernel(lams_ref, q_ref, k_ref, v_ref, g_ref, o_ref, s_ref, *, li, tq):
    seq = q_ref.shape[0]
    lane_blocks = tq // HEAD_W
    lam = _lam_from(lams_ref[...], li)
    lane = lax.broadcasted_iota(jnp.int32, (tq, HEAD_W), 1)
    r = lax.broadcasted_iota(jnp.int32, (2 * tq, tq), 0)
    c = lax.broadcasted_iota(jnp.int32, (2 * tq, tq), 1)
    visible = c <= jnp.where(r >= tq, r - tq, r)

    def fold(x, op):
        out = x[:, :HEAD_W]
        for b in range(1, lane_blocks):
            out = op(out, x[:, b * HEAD_W:(b + 1) * HEAD_W])
        return out

    slot = 0
    for qi in range(seq // tq):
        q = q_ref[qi * tq:(qi + 1) * tq, :]
        zero = jnp.zeros_like(q)
        qs = jnp.concatenate([jnp.where(lane < ATT_HD, q, zero),
                              jnp.where(lane >= ATT_HD, q, zero)], axis=0)
        m = None
        for j in range(qi + 1):
            s = _nt_dot(qs, k_ref[j * tq:(j + 1) * tq, :])
            if j == qi:
                s = jnp.where(visible, s, -jnp.inf)
            s_ref[slot + j] = s
            fm = fold(s, jnp.maximum)
            m = fm if m is None else jnp.maximum(m, fm)
        mb = jnp.broadcast_to(jnp.max(m, axis=-1, keepdims=True), m.shape)
        l = jnp.zeros((2 * tq, HEAD_W), F32)
        acc = jnp.zeros((2 * tq, HEAD_W), F32)
        for j in range(qi + 1):
            s = s_ref[slot + j]
            p = jnp.concatenate([jnp.exp2(s[:, b * HEAD_W:(b + 1) * HEAD_W] - mb)
                                 for b in range(lane_blocks)], axis=1)
            l = l + fold(p, jnp.add)
            acc = acc + jnp.dot(p.astype(BF16), v_ref[j * tq:(j + 1) * tq, :],
                                preferred_element_type=F32)
        slot += qi + 1
        o = acc / jnp.sum(l, axis=-1, keepdims=True)
        d = o[:tq] - lam * o[tq:]
        o_ref[qi * tq:(qi + 1) * tq, :] = (_rms(d, g_ref[...]) * (1.0 - _lambda_init(li))).astype(BF16)


def _attn_prompt(lams, aq, akb, avb, g_subln, li, batch, seq, tq):
    nq = seq // tq
    n = batch * seq
    per_head = pl.BlockSpec((seq, HEAD_W), lambda b, h: (b, h))
    return pl.pallas_call(
        functools.partial(_attn_prompt_kernel, li=li, tq=tq),
        grid=(batch, ATT_HEADS),
        in_specs=[pl.BlockSpec(lams.shape, lambda b, h: (0, 0)),
                  per_head, per_head, per_head,
                  pl.BlockSpec((1, HEAD_W), lambda b, h: (0, 0))],
        out_specs=per_head,
        out_shape=jax.ShapeDtypeStruct((n, ATT_WIDTH), BF16),
        scratch_shapes=[pltpu.VMEM((nq * (nq + 1) // 2, 2 * tq, tq), F32)],
        compiler_params=pltpu.CompilerParams(
            dimension_semantics=("arbitrary", "arbitrary"),
            vmem_limit_bytes=VMEM_LIMIT),
        name="attn_prompt",
    )(lams, aq, akb, avb, g_subln)


SCORE_ROWS = 16


def _attn_sample_kernel(pt_ref, lams_ref, q_ref, kn_ref, vn_ref, g_ref, *rest, li, n_pages):
    del pt_ref
    k_pages = rest[:n_pages]
    v_pages = rest[n_pages:2 * n_pages]
    o_ref = rest[2 * n_pages]
    pr = k_pages[0].shape[1]
    q = q_ref[0].astype(F32)
    kn = kn_ref[0]
    vn = vn_ref[0]
    row = lax.broadcasted_iota(jnp.int32, (SCORE_ROWS, HEAD_W), 0)
    lane = lax.broadcasted_iota(jnp.int32, (SCORE_ROWS, HEAD_W), 1)
    qrows = jnp.zeros((SCORE_ROWS, HEAD_W), F32)
    knrows = jnp.zeros((SCORE_ROWS, HEAD_W), F32)
    vnrows = jnp.zeros((SCORE_ROWS, HEAD_W), F32)
    for h in range(ATT_HEADS):
        cs = slice(h * HEAD_W, (h + 1) * HEAD_W)
        in_head = row // 2 == h
        qrows = jnp.where(in_head & (lane // ATT_HD == row % 2), q[:, cs], qrows)
        knrows = jnp.where(in_head, kn[:, cs], knrows)
        vnrows = jnp.where(in_head, vn[:, cs], vnrows)
    qb = qrows.astype(BF16)
    s_past = jnp.concatenate([_nt_dot(qb, kp[0].astype(BF16)) for kp in k_pages], axis=1)
    srow = lax.broadcasted_iota(jnp.int32, s_past.shape, 0)
    scol = lax.broadcasted_iota(jnp.int32, s_past.shape, 1)
    own = (scol % ATT_HEADS) == (srow // 2)
    s_past = jnp.where(own, s_past, -jnp.inf)
    s_new = jnp.sum(qrows * knrows, axis=-1, keepdims=True)
    m = jnp.maximum(jnp.max(s_past, axis=-1, keepdims=True), s_new)
    p = jnp.exp2(s_past - m)
    p_new = jnp.exp2(s_new - m)
    l = jnp.sum(p, axis=-1, keepdims=True) + p_new
    acc = p_new * vnrows
    for j in range(n_pages):
        acc = acc + jnp.dot(p[:, j * pr:(j + 1) * pr].astype(BF16), v_pages[j][0].astype(BF16),
                            preferred_element_type=F32)
    o = acc / l
    lam = _lam_from(lams_ref[...], li)
    outs = []
    for h in range(ATT_HEADS):
        d = o[2 * h:2 * h + 1, :] - lam * o[2 * h + 1:2 * h + 2, :]
        outs.append(_rms(d, g_ref[...]) * (1.0 - _lambda_init(li)))
    o_ref[0] = jnp.concatenate(outs, axis=1).astype(BF16)


def _attn_sample(page_table, lams, aq, ak, av, g_subln, cache_k, cache_v, li):
    n_seq, n_pages = page_table.shape
    depth, n_pool, page = cache_k.shape[:3]
    pr = page * ATT_HEADS
    ck = cache_k.reshape(depth * n_pool, pr, HEAD_W)
    cv = cache_v.reshape(depth * n_pool, pr, HEAD_W)
    base = li * n_pool

    def page_spec(j):
        return pl.BlockSpec((1, pr, HEAD_W), lambda s, pt: (base + pt[s, j], 0, 0))

    tok = pl.BlockSpec((1, 1, ATT_WIDTH), lambda s, pt: (s, 0, 0))
    grid_spec = pltpu.PrefetchScalarGridSpec(
        num_scalar_prefetch=1,
        grid=(n_seq,),
        in_specs=[pl.BlockSpec(lams.shape, lambda s, pt: (0, 0)), tok, tok, tok,
                  pl.BlockSpec((1, HEAD_W), lambda s, pt: (0, 0))]
        + [page_spec(j) for j in range(n_pages)] * 2,
        out_specs=tok,
    )
    out = pl.pallas_call(
        functools.partial(_attn_sample_kernel, li=li, n_pages=n_pages),
        grid_spec=grid_spec,
        out_shape=jax.ShapeDtypeStruct((n_seq, 1, ATT_WIDTH), BF16),
        compiler_params=pltpu.CompilerParams(dimension_semantics=("arbitrary",),
                                             vmem_limit_bytes=VMEM_LIMIT),
        name="attn_sample",
    )(page_table, lams, aq.reshape(n_seq, 1, ATT_WIDTH), ak.reshape(n_seq, 1, ATT_WIDTH),
      av.reshape(n_seq, 1, ATT_WIDTH), g_subln, *([ck] * n_pages), *([cv] * n_pages))
    return out.reshape(n_seq, ATT_WIDTH)


def _split3(x):
    hi = x.astype(BF16)
    r = x - hi.astype(F32)
    mid = r.astype(BF16)
    lo = (r - mid.astype(F32)).astype(BF16)
    return hi, mid, lo


def _hgrn_prompt_kernel(tri_ref, q_ref, k_ref, v_ref, lf_ref, gate_ref, grec_ref,
                        o_ref, s_ref, st_ref, b_ref, acc_ref, *, tt):
    t = pl.program_id(1)
    nt = pl.num_programs(1)
    c = HGRN_CHUNK

    @pl.when(t == 0)
    def _():
        st_ref[...] = jnp.zeros_like(st_ref)

    tri = tri_ref[...]
    for r0 in range(0, tt, HGRN_CUM_BLOCK):
        lf2 = lf_ref[r0:r0 + HGRN_CUM_BLOCK, :] * LOG2E
        hi, mid, lo = _split3(lf2)
        b_ref[r0:r0 + HGRN_CUM_BLOCK, :] = (
            jnp.dot(tri, hi, preferred_element_type=F32)
            + jnp.dot(tri, mid, preferred_element_type=F32)
            + jnp.dot(tri, lo, preferred_element_type=F32))

    rowi = lax.broadcasted_iota(jnp.int32, (c, c), 0)
    coli = lax.broadcasted_iota(jnp.int32, (c, c), 1)
    pair = jnp.where(coli > rowi, -1, BAND + len(HGRN_LEVELS) - 1)
    for level in reversed(range(len(HGRN_LEVELS))):
        blk = HGRN_LEVELS[level]
        cls = rowi - coli if level == 0 else BAND + level - 1
        pair = jnp.where((rowi // blk == coli // blk) & (coli <= rowi), cls, pair)

    def chunk_body(ci, carry):
        r0 = pl.multiple_of(ci * c, c)
        heads = [slice(h * REC_DK, (h + 1) * REC_DK) for h in range(REC_HEADS)]
        qs = [q_ref[pl.ds(r0, c), cs].astype(F32) for cs in heads]
        ks = [k_ref[pl.ds(r0, c), cs].astype(F32) for cs in heads]
        bs = [b_ref[pl.ds(r0, c), cs] for cs in heads]
        o_state, cross = [], []
        for h, cs in enumerate(heads):
            q, k, b = qs[h], ks[h], bs[h]
            vb = v_ref[pl.ds(r0, c), cs]
            st = st_ref[h]
            o_state.append(_nt_dot((q * jnp.exp2(b)).astype(BF16), st.astype(BF16)))
            bl = b[c - 1:c, :]
            kd = (k * jnp.exp2(bl - b)).astype(BF16)
            st_ref[h] = st * jnp.exp2(bl) + _tn_dot(vb, kd)
            per_level = []
            for half in HGRN_LEVELS:
                blk = 2 * half
                ref = jnp.concatenate(
                    [jnp.broadcast_to(b[g0 * blk + half - 1:g0 * blk + half, :], (blk, REC_DK))
                     for g0 in range(c // blk)], axis=0)
                d = b - ref
                qe = jnp.minimum(d, 0.0)
                qf = (q * jnp.exp2(qe)).astype(BF16)
                kf = (k * jnp.exp2(qe - d)).astype(BF16)
                per_level.append(_nt_dot(qf, kf))
            cross.append(per_level)
        for h, cs in enumerate(heads):
            q = qs[h]
            f = jnp.exp(lf_ref[pl.ds(r0, c), cs])
            g = ks[h]
            a = jnp.zeros((c, c), F32)
            for dist in range(BAND):
                if dist > 0:
                    g3 = g.reshape(c // SUBLANES, SUBLANES, REC_DK)
                    g = f * pltpu.roll(g3, 1, 1).reshape(c, REC_DK)
                w = jnp.sum(q * g, axis=-1, keepdims=True)
                a = jnp.where(pair == dist, w, a)
            for level in range(len(HGRN_LEVELS)):
                a = jnp.where(pair == BAND + level, cross[h][level], a)
            vb = v_ref[pl.ds(r0, c), cs]
            acc_ref[pl.ds(r0, c), cs] = o_state[h] + jnp.dot(a.astype(BF16), vb,
                                                             preferred_element_type=F32)
        return carry

    lax.fori_loop(0, tt // c, chunk_body, 0, unroll=True)

    o_all = acc_ref[...]
    o_ref[...] = (_rms(o_all, grec_ref[...]) * gate_ref[...].astype(F32)).astype(BF16)

    @pl.when(t == nt - 1)
    def _():
        for h in range(REC_HEADS):
            s_ref[0, h] = st_ref[h].T


def _hgrn_tri():
    i = np.arange(HGRN_CUM_BLOCK)
    m = (i[:, None] >= i[None, :]) & ((i[:, None] // HGRN_CHUNK) == (i[None, :] // HGRN_CHUNK))
    return jnp.asarray(m.astype(np.float32), dtype=BF16)


def _hgrn_prompt(rq, rk, rv, lf, rg, g_rec, batch, seq, tt):
    nt = seq // tt
    n = batch * seq
    tile = pl.BlockSpec((tt, REC_WIDTH), lambda b, t: (b * nt + t, 0))
    return pl.pallas_call(
        functools.partial(_hgrn_prompt_kernel, tt=tt),
        grid=(batch, nt),
        in_specs=[pl.BlockSpec((HGRN_CUM_BLOCK, HGRN_CUM_BLOCK), lambda b, t: (0, 0)),
                  tile, tile, tile, tile, tile,
                  pl.BlockSpec((1, REC_WIDTH), lambda b, t: (0, 0))],
        out_specs=[tile, pl.BlockSpec((1, REC_HEADS, REC_DK, REC_DK), lambda b, t: (b, 0, 0, 0))],
        out_shape=[jax.ShapeDtypeStruct((n, REC_WIDTH), BF16),
                   jax.ShapeDtypeStruct((batch, REC_HEADS, REC_DK, REC_DK), F32)],
        scratch_shapes=[pltpu.VMEM((REC_HEADS, REC_DK, REC_DK), F32),
                        pltpu.VMEM((tt, REC_WIDTH), F32),
                        pltpu.VMEM((tt, REC_WIDTH), F32)],
        compiler_params=pltpu.CompilerParams(dimension_semantics=("arbitrary", "arbitrary"),
                                             vmem_limit_bytes=VMEM_LIMIT),
        name="hgrn_prompt",
    )(_hgrn_tri(), rq, rk, rv, lf, rg, g_rec)


def _hgrn_sample_kernel(q_ref, k_ref, v_ref, lf_ref, gate_ref, grec_ref, s0_ref, o_ref, s_ref, *, group):
    rowi = lax.broadcasted_iota(jnp.int32, (group, REC_DK), 0)
    outs = []
    for h in range(REC_HEADS):
        cs = slice(h * REC_DK, (h + 1) * REC_DK)
        q = q_ref[:, cs]
        k = k_ref[:, cs]
        v = v_ref[:, cs]
        f = jnp.exp(lf_ref[:, cs])
        o_h = jnp.zeros((group, REC_DK), F32)
        for n in range(group):
            st = s0_ref[n, h].T
            vn = jnp.where(rowi == n, v, jnp.zeros_like(v))
            st = st * f[n:n + 1, :] + _tn_dot(vn, k)
            s_ref[n, h] = st.T
            o_h = jnp.where(rowi == n, _nt_dot(q, st.astype(BF16)), o_h)
        outs.append(o_h)
    o = jnp.concatenate(outs, axis=1)
    o_ref[...] = (_rms(o, grec_ref[...]) * gate_ref[...].astype(F32)).astype(BF16)


def _hgrn_sample(rq, rk, rv, lf, rg, g_rec, s0, group):
    n = rq.shape[0]
    tile = pl.BlockSpec((group, REC_WIDTH), lambda i: (i, 0))
    st = pl.BlockSpec((group, REC_HEADS, REC_DK, REC_DK), lambda i: (i, 0, 0, 0))
    return pl.pallas_call(
        functools.partial(_hgrn_sample_kernel, group=group),
        grid=(n // group,),
        in_specs=[tile, tile, tile, tile, tile, pl.BlockSpec((1, REC_WIDTH), lambda i: (0, 0)), st],
        out_specs=[tile, st],
        out_shape=[jax.ShapeDtypeStruct((n, REC_WIDTH), BF16),
                   jax.ShapeDtypeStruct(s0.shape, F32)],
        compiler_params=pltpu.CompilerParams(dimension_semantics=("arbitrary",),
                                             vmem_limit_bytes=VMEM_LIMIT),
        name="hgrn_sample",
    )(rq, rk, rv, lf, rg, g_rec, s0)


FF_BLOCK = 1024


def _tail_kernel(x_ref, att_ref, rec_ref, p_ref, wo_ref, gffn_ref, wup_ref, wdn_ref,
                 wg_ref, wp_ref, gfin_ref, y_ref, *, final):
    mix = (jnp.dot(att_ref[...], wo_ref[:ATT_WIDTH, :], preferred_element_type=F32)
           + jnp.dot(rec_ref[...], wo_ref[ATT_WIDTH:, :], preferred_element_type=F32))
    x = x_ref[...] + mix
    h = _rms(x, gffn_ref[...]).astype(BF16)
    for c0 in range(0, D_FF, FF_BLOCK):
        up = jnp.dot(h, wup_ref[:, c0:c0 + FF_BLOCK], preferred_element_type=F32)
        act = jnp.square(jnp.maximum(up, 0.0)).astype(BF16)
        x = x + jnp.dot(act, wdn_ref[c0:c0 + FF_BLOCK, :], preferred_element_type=F32)
    gate = jax.nn.sigmoid(jnp.dot(x.astype(BF16), wg_ref[...], preferred_element_type=F32))
    emb = jnp.dot(p_ref[...].astype(BF16), wp_ref[...], preferred_element_type=F32)
    x = x + gate * emb
    y_ref[...] = _rms(x, gfin_ref[...]) if final else x


def _tail(x, att, rec, p, w_out, g_ffn, w_up, w_down, w_gate, w_proj, g_final, final, tm):
    n = x.shape[0]
    row = lambda i: (i, 0)
    const = lambda i: (0, 0)

    def resident(a):
        return pl.BlockSpec(a.shape, const, pipeline_mode=pl.Buffered(1))

    return pl.pallas_call(
        functools.partial(_tail_kernel, final=final),
        grid=(n // tm,),
        in_specs=[pl.BlockSpec((tm, D_MODEL), row),
                  pl.BlockSpec((tm, ATT_WIDTH), row),
                  pl.BlockSpec((tm, REC_WIDTH), row),
                  pl.BlockSpec((tm, PLE_DIM), row),
                  resident(w_out), resident(g_ffn), resident(w_up), resident(w_down),
                  resident(w_gate), resident(w_proj), resident(g_final)],
        out_specs=pl.BlockSpec((tm, D_MODEL), row),
        out_shape=jax.ShapeDtypeStruct((n, D_MODEL), F32),
        compiler_params=pltpu.CompilerParams(dimension_semantics=("arbitrary",),
                                             vmem_limit_bytes=VMEM_LIMIT),
        name="tail",
    )(x, att, rec, p, w_out, g_ffn, w_up, w_down, w_gate, w_proj, g_final)


def kernel(x_prompt, x_sample, cache_k, cache_v, state_hgrn, page_table, p_prompt, p_sample,
           w_in, lambda_q1, lambda_k1, lambda_q2, lambda_k2, g_subln, hgrn_lb, g_rec, w_out,
           g_mix, g_ffn, w_up, w_down, w_ple_gate, w_ple_proj, g_final):
    batch, seq, d = x_prompt.shape
    n_seq = x_sample.shape[0]
    depth = w_in.shape[0]
    assert d == D_MODEL and w_in.shape[2] == N_PROJ * ATT_WIDTH and x_sample.shape[1] == 1
    assert cache_k.shape[3:] == (ATT_HEADS, HEAD_W) and state_hgrn.shape[2:] == (REC_HEADS, REC_DK, REC_DK)
    n_p = batch * seq

    hp = x_prompt.reshape(n_p, d)
    hs = x_sample.reshape(n_seq, d)
    row2 = lambda a: a.reshape(1, -1)
    outs = [[] for _ in range(6)]
    for li in range(depth):
        last = li == depth - 1
        w_in_b = w_in[li].astype(BF16)
        w_out_b = w_out[li].astype(BF16)
        w_up_b = w_up[li].astype(BF16)
        w_dn_b = w_down[li].astype(BF16)
        w_g_b = w_ple_gate[li].astype(BF16)
        w_p_b = w_ple_proj[li].astype(BF16)
        lams = jnp.stack([lambda_q1[li], lambda_k1[li], lambda_q2[li], lambda_k2[li]], axis=0)
        g_sub = row2(g_subln[li])
        g_r = row2(g_rec[li])
        tail_w = (w_out_b, row2(g_ffn[li]), w_up_b, w_dn_b, w_g_b, w_p_b, row2(g_final))

        aq, ak, av, akb, avb, rq, lf, rk, rv, rg = _inproj(hp, row2(g_mix[li]), w_in_b, hgrn_lb, li, tm=512)
        att = _attn_prompt(lams, aq, akb, avb, g_sub, li, batch, seq, tq=256)
        rec, s_p = _hgrn_prompt(rq, rk, rv, lf, rg, g_r, batch, seq, tt=512)
        hp = _tail(hp, att, rec, p_prompt[li].reshape(n_p, PLE_DIM), *tail_w, final=last, tm=512)
        outs[0].append(ak.reshape(batch, seq, ATT_HEADS, HEAD_W))
        outs[1].append(av.reshape(batch, seq, ATT_HEADS, HEAD_W))
        outs[2].append(s_p)

        aq, ak, av, _, _, rq, lf, rk, rv, rg = _inproj(hs, row2(g_mix[li]), w_in_b, hgrn_lb, li, tm=n_seq)
        att = _attn_sample(page_table, lams, aq, ak, av, g_sub, cache_k, cache_v, li)
        rec, s_s = _hgrn_sample(rq, rk, rv, lf, rg, g_r, state_hgrn[li], group=16)
        hs = _tail(hs, att, rec, p_sample[li].reshape(n_seq, PLE_DIM), *tail_w, final=last, tm=n_seq)
        outs[3].append(ak.reshape(n_seq, 1, ATT_HEADS, HEAD_W))
        outs[4].append(av.reshape(n_seq, 1, ATT_HEADS, HEAD_W))
        outs[5].append(s_s)

    k_p, v_p, s_p, k_s, v_s, s_s = [jnp.stack(o, axis=0) for o in outs]
    return (hp.reshape(batch, seq, d), hs.reshape(n_seq, 1, d), k_p, v_p, s_p, k_s, v_s, s_s)
```

```python
import functools
import math

import jax
import jax.numpy as jnp
import numpy as np
from jax import lax
from jax.experimental import pallas as pl
from jax.experimental.pallas import tpu as pltpu

F32 = jnp.float32
BF16 = jnp.bfloat16

D_MODEL = 1024
ATT_HEADS = 4
ATT_HD = 64
HEAD_W = 2 * ATT_HD
ATT_WIDTH = ATT_HEADS * HEAD_W
REC_HEADS = 4
REC_DK = 128
REC_WIDTH = REC_HEADS * REC_DK
N_PROJ = 7
PLE_DIM = 256
D_FF = 4 * D_MODEL
EPS = 1e-6
LOG2E = math.log2(math.e)
ATT_SCALE = ATT_HD ** -0.5 * LOG2E

V7X_VMEM_BYTES = 64 * 1024 * 1024
VMEM_LIMIT = 52 * 1024 * 1024

HGRN_CHUNK = 64
HGRN_CUM_BLOCK = 256
BAND = 4
HGRN_LEVELS = (4, 8, 16, 32)
SUBLANES = 8


def _lambda_init(li):
    return 0.8 - 0.6 * math.exp(-0.3 * li)


def _rms(x, g):
    return x * lax.rsqrt(jnp.mean(x * x, axis=-1, keepdims=True) + EPS) * g


def _silu(x):
    return x * jax.nn.sigmoid(x)


def _nt_dot(a, b):
    return lax.dot_general(a, b, (((1,), (1,)), ((), ())), preferred_element_type=F32)


def _tn_dot(a, b):
    return lax.dot_general(a, b, (((0,), (0,)), ((), ())), preferred_element_type=F32)


def _lam_from(lams, li):
    a = jnp.sum(lams[0:1, :] * lams[1:2, :], axis=-1, keepdims=True)
    b = jnp.sum(lams[2:3, :] * lams[3:4, :], axis=-1, keepdims=True)
    return jnp.exp(a) - jnp.exp(b) + _lambda_init(li)


def _inproj_kernel(x_ref, g_ref, w_ref, lb_ref,
                   aq_ref, ak_ref, av_ref, akb_ref, avb_ref,
                   rq_ref, lf_ref, rk_ref, rv_ref, rg_ref, *, li):
    h = _rms(x_ref[...], g_ref[...]).astype(BF16)

    def proj(i):
        return jnp.dot(h, w_ref[:, i * ATT_WIDTH:(i + 1) * ATT_WIDTH], preferred_element_type=F32)

    aq_ref[...] = (proj(0) * ATT_SCALE).astype(BF16)
    tm = x_ref.shape[0]
    for full_ref, half_ref, val in ((ak_ref, akb_ref, proj(1)), (av_ref, avb_ref, proj(2))):
        half_ref[...] = val.astype(BF16)
        for hd in range(ATT_HEADS):
            full_ref[pl.ds(hd, tm, stride=ATT_HEADS), :] = val[:, hd * HEAD_W:(hd + 1) * HEAD_W]
    rq_ref[...] = _silu(proj(3)).astype(BF16)
    lbp = lb_ref[...]
    e = jnp.exp(lbp - jnp.max(lbp, axis=0, keepdims=True))
    lb = jnp.sum(e[:li + 1], axis=0, keepdims=True) / jnp.sum(e, axis=0, keepdims=True)
    f = lb + (1.0 - lb) * jax.nn.sigmoid(proj(4))
    lf_ref[...] = jnp.log(f)
    rk_ref[...] = (1.0 - f).astype(BF16)
    rv_ref[...] = proj(5).astype(BF16)
    rg_ref[...] = _silu(proj(6)).astype(BF16)


def _inproj(x, g_mix, w_in, hgrn_lb, li, tm):
    n = x.shape[0]
    grid = (n // tm,)
    row = lambda i: (i, 0)
    const = lambda i: (0, 0)
    wide = pl.BlockSpec((tm, ATT_WIDTH), row)
    heads = pl.BlockSpec((tm * ATT_HEADS, HEAD_W), row)
    flat = lambda dt: jax.ShapeDtypeStruct((n, ATT_WIDTH), dt)
    by_head = jax.ShapeDtypeStruct((n * ATT_HEADS, HEAD_W), F32)
    return pl.pallas_call(
        functools.partial(_inproj_kernel, li=li),
        grid=grid,
        in_specs=[pl.BlockSpec((tm, D_MODEL), row),
                  pl.BlockSpec((1, D_MODEL), const),
                  pl.BlockSpec(w_in.shape, const, pipeline_mode=pl.Buffered(1)),
                  pl.BlockSpec(hgrn_lb.shape, const)],
        out_specs=[wide, heads, heads] + [wide] * 7,
        out_shape=[flat(BF16), by_head, by_head, flat(BF16), flat(BF16),
                   flat(BF16), flat(F32), flat(BF16), flat(BF16), flat(BF16)],
        compiler_params=pltpu.CompilerParams(dimension_semantics=("arbitrary",),
                                             vmem_limit_bytes=VMEM_LIMIT),
        name="inproj",
    )(x, g_mix, w_in, hgrn_lb)


def _attn_prompt_kernel(lams_ref, q_ref, k_ref, v_ref, g_ref, o_ref, s_ref, *, li, tq):
    seq = q_ref.shape[0]
    lane_blocks = tq // HEAD_W
    lam = _lam_from(lams_ref[...], li)
    lane = lax.broadcasted_iota(jnp.int32, (tq, HEAD_W), 1)
    r = lax.broadcasted_iota(jnp.int32, (2 * tq, tq), 0)
    c = lax.broadcasted_iota(jnp.int32, (2 * tq, tq), 1)
    visible = c <= jnp.where(r >= tq, r - tq, r)

    def fold(x, op):
        out = x[:, :HEAD_W]
        for b in range(1, lane_blocks):
            out = op(out, x[:, b * HEAD_W:(b + 1) * HEAD_W])
        return out

    ring = s_ref.shape[0]

    def pass1(hh, qi, base):
        cs = slice(hh * HEAD_W, (hh + 1) * HEAD_W)
        q = q_ref[qi * tq:(qi + 1) * tq, cs]
        zero = jnp.zeros_like(q)
        qs = jnp.concatenate([jnp.where(lane < ATT_HD, q, zero),
                              jnp.where(lane >= ATT_HD, q, zero)], axis=0)
        m = None
        for j in range(qi + 1):
            s = _nt_dot(qs, k_ref[j * tq:(j + 1) * tq, cs])
            if j == qi:
                s = jnp.where(visible, s, -jnp.inf)
            s_ref[(base + j) % ring] = s
            fm = fold(s, jnp.maximum)
            m = fm if m is None else jnp.maximum(m, fm)
        return jnp.broadcast_to(jnp.max(m, axis=-1, keepdims=True), m.shape)

    def pass2(hh, qi, base, mb):
        cs = slice(hh * HEAD_W, (hh + 1) * HEAD_W)
        l = jnp.zeros((2 * tq, HEAD_W), F32)
        acc = jnp.zeros((2 * tq, HEAD_W), F32)
        for j in range(qi + 1):
            s = s_ref[(base + j) % ring]
            p = jnp.concatenate([jnp.exp2(s[:, b * HEAD_W:(b + 1) * HEAD_W] - mb)
                                 for b in range(lane_blocks)], axis=1)
            l = l + fold(p, jnp.add)
            acc = acc + jnp.dot(p.astype(BF16), v_ref[j * tq:(j + 1) * tq, cs],
                                preferred_element_type=F32)
        o = acc / jnp.sum(l, axis=-1, keepdims=True)
        d = o[:tq] - lam * o[tq:]
        o_ref[qi * tq:(qi + 1) * tq, cs] = (_rms(d, g_ref[...]) * (1.0 - _lambda_init(li))).astype(BF16)

    pending = None
    base = 0
    for qi in range(seq // tq):
        for hh in range(q_ref.shape[1] // HEAD_W):
            mb = pass1(hh, qi, base)
            if pending is not None:
                pass2(*pending)
            pending = (hh, qi, base, mb)
            base += qi + 1
    pass2(*pending)


def _attn_prompt(lams, aq, akb, avb, g_subln, li, batch, seq, tq, heads_per_step):
    nq = seq // tq
    n = batch * seq
    group = pl.BlockSpec((seq, heads_per_step * HEAD_W), lambda b, h: (b, h))
    return pl.pallas_call(
        functools.partial(_attn_prompt_kernel, li=li, tq=tq),
        grid=(batch, ATT_HEADS // heads_per_step),
        in_specs=[pl.BlockSpec(lams.shape, lambda b, h: (0, 0)),
                  group, group, group,
                  pl.BlockSpec((1, HEAD_W), lambda b, h: (0, 0))],
        out_specs=group,
        out_shape=jax.ShapeDtypeStruct((n, ATT_WIDTH), BF16),
        scratch_shapes=[pltpu.VMEM((2 * nq, 2 * tq, tq), F32)],
        compiler_params=pltpu.CompilerParams(
            dimension_semantics=("arbitrary", "arbitrary"),
            vmem_limit_bytes=VMEM_LIMIT),
        name="attn_prompt",
    )(lams, aq, akb, avb, g_subln)


SCORE_ROWS = 16


def _attn_sample_kernel(pt_ref, lams_ref, q_ref, kn_ref, vn_ref, g_ref, *rest, li, n_pages):
    del pt_ref
    k_pages = rest[:n_pages]
    v_pages = rest[n_pages:2 * n_pages]
    o_ref = rest[2 * n_pages]
    pr = k_pages[0].shape[1]
    q = q_ref[0].astype(F32)
    kn = kn_ref[0]
    vn = vn_ref[0]
    row = lax.broadcasted_iota(jnp.int32, (SCORE_ROWS, HEAD_W), 0)
    lane = lax.broadcasted_iota(jnp.int32, (SCORE_ROWS, HEAD_W), 1)
    qrows = jnp.zeros((SCORE_ROWS, HEAD_W), F32)
    knrows = jnp.zeros((SCORE_ROWS, HEAD_W), F32)
    vnrows = jnp.zeros((SCORE_ROWS, HEAD_W), F32)
    for h in range(ATT_HEADS):
        cs = slice(h * HEAD_W, (h + 1) * HEAD_W)
        in_head = row // 2 == h
        qrows = jnp.where(in_head & (lane // ATT_HD == row % 2), q[:, cs], qrows)
        knrows = jnp.where(in_head, kn[:, cs], knrows)
        vnrows = jnp.where(in_head, vn[:, cs], vnrows)
    qb = qrows.astype(BF16)
    s_past = jnp.concatenate([_nt_dot(qb, kp[0].astype(BF16)) for kp in k_pages], axis=1)
    srow = lax.broadcasted_iota(jnp.int32, s_past.shape, 0)
    scol = lax.broadcasted_iota(jnp.int32, s_past.shape, 1)
    own = (scol % ATT_HEADS) == (srow // 2)
    s_past = jnp.where(own, s_past, -jnp.inf)
    s_new = jnp.sum(qrows * knrows, axis=-1, keepdims=True)
    m = jnp.maximum(jnp.max(s_past, axis=-1, keepdims=True), s_new)
    p = jnp.exp2(s_past - m)
    p_new = jnp.exp2(s_new - m)
    l = jnp.sum(p, axis=-1, keepdims=True) + p_new
    acc = p_new * vnrows
    for j in range(n_pages):
        acc = acc + jnp.dot(p[:, j * pr:(j + 1) * pr].astype(BF16), v_pages[j][0].astype(BF16),
                            preferred_element_type=F32)
    o = acc / l
    lam = _lam_from(lams_ref[...], li)
    outs = []
    for h in range(ATT_HEADS):
        d = o[2 * h:2 * h + 1, :] - lam * o[2 * h + 1:2 * h + 2, :]
        outs.append(_rms(d, g_ref[...]) * (1.0 - _lambda_init(li)))
    o_ref[0] = jnp.concatenate(outs, axis=1).astype(BF16)


def _attn_sample(page_table, lams, aq, ak, av, g_subln, cache_k, cache_v, li):
    n_seq, n_pages = page_table.shape
    depth, n_pool, page = cache_k.shape[:3]
    pr = page * ATT_HEADS
    ck = cache_k.reshape(depth * n_pool, pr, HEAD_W)
    cv = cache_v.reshape(depth * n_pool, pr, HEAD_W)
    base = li * n_pool

    def page_spec(j):
        return pl.BlockSpec((1, pr, HEAD_W), lambda s, pt: (base + pt[s, j], 0, 0))

    tok = pl.BlockSpec((1, 1, ATT_WIDTH), lambda s, pt: (s, 0, 0))
    grid_spec = pltpu.PrefetchScalarGridSpec(
        num_scalar_prefetch=1,
        grid=(n_seq,),
        in_specs=[pl.BlockSpec(lams.shape, lambda s, pt: (0, 0)), tok, tok, tok,
                  pl.BlockSpec((1, HEAD_W), lambda s, pt: (0, 0))]
        + [page_spec(j) for j in range(n_pages)] * 2,
        out_specs=tok,
    )
    out = pl.pallas_call(
        functools.partial(_attn_sample_kernel, li=li, n_pages=n_pages),
        grid_spec=grid_spec,
        out_shape=jax.ShapeDtypeStruct((n_seq, 1, ATT_WIDTH), BF16),
        compiler_params=pltpu.CompilerParams(dimension_semantics=("arbitrary",),
                                             vmem_limit_bytes=VMEM_LIMIT),
        name="attn_sample",
    )(page_table, lams, aq.reshape(n_seq, 1, ATT_WIDTH), ak.reshape(n_seq, 1, ATT_WIDTH),
      av.reshape(n_seq, 1, ATT_WIDTH), g_subln, *([ck] * n_pages), *([cv] * n_pages))
    return out.reshape(n_seq, ATT_WIDTH)


def _split3(x):
    hi = x.astype(BF16)
    r = x - hi.astype(F32)
    mid = r.astype(BF16)
    lo = (r - mid.astype(F32)).astype(BF16)
    return hi, mid, lo


def _hgrn_prompt_kernel(tri_ref, q_ref, k_ref, v_ref, lf_ref, gate_ref, grec_ref,
                        o_ref, s_ref, st_ref, b_ref, acc_ref, *, tt):
    t = pl.program_id(1)
    nt = pl.num_programs(1)
    c = HGRN_CHUNK

    @pl.when(t == 0)
    def _():
        st_ref[...] = jnp.zeros_like(st_ref)

    tri = tri_ref[...]
    for r0 in range(0, tt, HGRN_CUM_BLOCK):
        lf2 = lf_ref[r0:r0 + HGRN_CUM_BLOCK, :] * LOG2E
        hi, mid, lo = _split3(lf2)
        b_ref[r0:r0 + HGRN_CUM_BLOCK, :] = (
            jnp.dot(tri, hi, preferred_element_type=F32)
            + jnp.dot(tri, mid, preferred_element_type=F32)
            + jnp.dot(tri, lo, preferred_element_type=F32))

    rowi = lax.broadcasted_iota(jnp.int32, (c, c), 0)
    coli = lax.broadcasted_iota(jnp.int32, (c, c), 1)
    pair = jnp.where(coli > rowi, -1, BAND + len(HGRN_LEVELS) - 1)
    for level in reversed(range(len(HGRN_LEVELS))):
        blk = HGRN_LEVELS[level]
        cls = rowi - coli if level == 0 else BAND + level - 1
        pair = jnp.where((rowi // blk == coli // blk) & (coli <= rowi), cls, pair)

    def chunk_body(ci, carry):
        r0 = pl.multiple_of(ci * c, c)
        heads = [slice(h * REC_DK, (h + 1) * REC_DK) for h in range(REC_HEADS)]
        qs = [q_ref[pl.ds(r0, c), cs].astype(F32) for cs in heads]
        ks = [k_ref[pl.ds(r0, c), cs].astype(F32) for cs in heads]
        bs = [b_ref[pl.ds(r0, c), cs] for cs in heads]
        o_state, cross = [], []
        for h, cs in enumerate(heads):
            q, k, b = qs[h], ks[h], bs[h]
            vb = v_ref[pl.ds(r0, c), cs]
            st = st_ref[h]
            o_state.append(_nt_dot((q * jnp.exp2(b)).astype(BF16), st.astype(BF16)))
            bl = b[c - 1:c, :]
            kd = (k * jnp.exp2(bl - b)).astype(BF16)
            st_ref[h] = st * jnp.exp2(bl) + _tn_dot(vb, kd)
            per_level = []
            for half in HGRN_LEVELS:
                blk = 2 * half
                ref = jnp.concatenate(
                    [jnp.broadcast_to(b[g0 * blk + half - 1:g0 * blk + half, :], (blk, REC_DK))
                     for g0 in range(c // blk)], axis=0)
                d = b - ref
                qe = jnp.minimum(d, 0.0)
                qf = (q * jnp.exp2(qe)).astype(BF16)
                kf = (k * jnp.exp2(qe - d)).astype(BF16)
                per_level.append(_nt_dot(qf, kf))
            cross.append(per_level)
        for h, cs in enumerate(heads):
            q = qs[h]
            f = jnp.exp(lf_ref[pl.ds(r0, c), cs])
            g = ks[h]
            a = jnp.zeros((c, c), F32)
            for dist in range(BAND):
                if dist > 0:
                    g3 = g.reshape(c // SUBLANES, SUBLANES, REC_DK)
                    g = f * pltpu.roll(g3, 1, 1).reshape(c, REC_DK)
                w = jnp.sum(q * g, axis=-1, keepdims=True)
                a = jnp.where(pair == dist, w, a)
            for level in range(len(HGRN_LEVELS)):
                a = jnp.where(pair == BAND + level, cross[h][level], a)
            vb = v_ref[pl.ds(r0, c), cs]
            acc_ref[pl.ds(r0, c), cs] = o_state[h] + jnp.dot(a.astype(BF16), vb,
                                                             preferred_element_type=F32)
        return carry

    lax.fori_loop(0, tt // c, chunk_body, 0, unroll=True)

    o_all = acc_ref[...]
    o_ref[...] = (_rms(o_all, grec_ref[...]) * gate_ref[...].astype(F32)).astype(BF16)

    @pl.when(t == nt - 1)
    def _():
        for h in range(REC_HEADS):
            s_ref[0, h] = st_ref[h].T


def _hgrn_tri():
    i = np.arange(HGRN_CUM_BLOCK)
    m = (i[:, None] >= i[None, :]) & ((i[:, None] // HGRN_CHUNK) == (i[None, :] // HGRN_CHUNK))
    return jnp.asarray(m.astype(np.float32), dtype=BF16)


def _hgrn_prompt(rq, rk, rv, lf, rg, g_rec, batch, seq, tt):
    nt = seq // tt
    n = batch * seq
    tile = pl.BlockSpec((tt, REC_WIDTH), lambda b, t: (b * nt + t, 0))
    return pl.pallas_call(
        functools.partial(_hgrn_prompt_kernel, tt=tt),
        grid=(batch, nt),
        in_specs=[pl.BlockSpec((HGRN_CUM_BLOCK, HGRN_CUM_BLOCK), lambda b, t: (0, 0)),
                  tile, tile, tile, tile, tile,
                  pl.BlockSpec((1, REC_WIDTH), lambda b, t: (0, 0))],
        out_specs=[tile, pl.BlockSpec((1, REC_HEADS, REC_DK, REC_DK), lambda b, t: (b, 0, 0, 0))],
        out_shape=[jax.ShapeDtypeStruct((n, REC_WIDTH), BF16),
                   jax.ShapeDtypeStruct((batch, REC_HEADS, REC_DK, REC_DK), F32)],
        scratch_shapes=[pltpu.VMEM((REC_HEADS, REC_DK, REC_DK), F32),
                        pltpu.VMEM((tt, REC_WIDTH), F32),
                        pltpu.VMEM((tt, REC_WIDTH), F32)],
        compiler_params=pltpu.CompilerParams(dimension_semantics=("arbitrary", "arbitrary"),
                                             vmem_limit_bytes=VMEM_LIMIT),
        name="hgrn_prompt",
    )(_hgrn_tri(), rq, rk, rv, lf, rg, g_rec)


def _hgrn_sample_kernel(q_ref, k_ref, v_ref, lf_ref, gate_ref, grec_ref, s0_ref, o_ref, s_ref, *, group):
    rowi = lax.broadcasted_iota(jnp.int32, (group, REC_DK), 0)
    outs = []
    for h in range(REC_HEADS):
        cs = slice(h * REC_DK, (h + 1) * REC_DK)
        q = q_ref[:, cs]
        k = k_ref[:, cs]
        v = v_ref[:, cs]
        f = jnp.exp(lf_ref[:, cs])
        outer = []
        for n in range(group):
            vn = jnp.where(rowi == n, v, jnp.zeros_like(v))
            outer.append(_tn_dot(vn, k))
        new = []
        for n in range(group):
            st = s0_ref[n, h].T * f[n:n + 1, :] + outer[n]
            s_ref[n, h] = st.T
            new.append(st.astype(BF16))
        o_h = jnp.zeros((group, REC_DK), F32)
        for n in range(group):
            o_h = jnp.where(rowi == n, _nt_dot(q, new[n]), o_h)
        outs.append(o_h)
    o = jnp.concatenate(outs, axis=1)
    o_ref[...] = (_rms(o, grec_ref[...]) * gate_ref[...].astype(F32)).astype(BF16)


def _hgrn_sample(rq, rk, rv, lf, rg, g_rec, s0, group):
    n = rq.shape[0]
    tile = pl.BlockSpec((group, REC_WIDTH), lambda i: (i, 0))
    st = pl.BlockSpec((group, REC_HEADS, REC_DK, REC_DK), lambda i: (i, 0, 0, 0))
    return pl.pallas_call(
        functools.partial(_hgrn_sample_kernel, group=group),
        grid=(n // group,),
        in_specs=[tile, tile, tile, tile, tile, pl.BlockSpec((1, REC_WIDTH), lambda i: (0, 0)), st],
        out_specs=[tile, st],
        out_shape=[jax.ShapeDtypeStruct((n, REC_WIDTH), BF16),
                   jax.ShapeDtypeStruct(s0.shape, F32)],
        compiler_params=pltpu.CompilerParams(dimension_semantics=("arbitrary",),
                                             vmem_limit_bytes=VMEM_LIMIT),
        name="hgrn_sample",
    )(rq, rk, rv, lf, rg, g_rec, s0)


FF_BLOCK = 1024


def _tail_kernel(x_ref, att_ref, rec_ref, p_ref, wo_ref, gffn_ref, wup_ref, wdn_ref,
                 wg_ref, wp_ref, gfin_ref, y_ref, *, final):
    mix = (jnp.dot(att_ref[...], wo_ref[:ATT_WIDTH, :], preferred_element_type=F32)
           + jnp.dot(rec_ref[...], wo_ref[ATT_WIDTH:, :], preferred_element_type=F32))
    x = x_ref[...] + mix
    h = _rms(x, gffn_ref[...]).astype(BF16)
    for c0 in range(0, D_FF, FF_BLOCK):
        up = jnp.dot(h, wup_ref[:, c0:c0 + FF_BLOCK], preferred_element_type=F32)
        act = jnp.square(jnp.maximum(up, 0.0)).astype(BF16)
        x = x + jnp.dot(act, wdn_ref[c0:c0 + FF_BLOCK, :], preferred_element_type=F32)
    gate = jax.nn.sigmoid(jnp.dot(x.astype(BF16), wg_ref[...], preferred_element_type=F32))
    emb = jnp.dot(p_ref[...].astype(BF16), wp_ref[...], preferred_element_type=F32)
    x = x + gate * emb
    y_ref[...] = _rms(x, gfin_ref[...]) if final else x


def _tail(x, att, rec, p, w_out, g_ffn, w_up, w_down, w_gate, w_proj, g_final, final, tm):
    n = x.shape[0]
    row = lambda i: (i, 0)
    const = lambda i: (0, 0)

    def resident(a):
        return pl.BlockSpec(a.shape, const, pipeline_mode=pl.Buffered(1))

    return pl.pallas_call(
        functools.partial(_tail_kernel, final=final),
        grid=(n // tm,),
        in_specs=[pl.BlockSpec((tm, D_MODEL), row),
                  pl.BlockSpec((tm, ATT_WIDTH), row),
                  pl.BlockSpec((tm, REC_WIDTH), row),
                  pl.BlockSpec((tm, PLE_DIM), row),
                  resident(w_out), resident(g_ffn), resident(w_up), resident(w_down),
                  resident(w_gate), resident(w_proj), resident(g_final)],
        out_specs=pl.BlockSpec((tm, D_MODEL), row),
        out_shape=jax.ShapeDtypeStruct((n, D_MODEL), F32),
        compiler_params=pltpu.CompilerParams(dimension_semantics=("arbitrary",),
                                             vmem_limit_bytes=VMEM_LIMIT),
        name="tail",
    )(x, att, rec, p, w_out, g_ffn, w_up, w_down, w_gate, w_proj, g_final)


def kernel(x_prompt, x_sample, cache_k, cache_v, state_hgrn, page_table, p_prompt, p_sample,
           w_in, lambda_q1, lambda_k1, lambda_q2, lambda_k2, g_subln, hgrn_lb, g_rec, w_out,
           g_mix, g_ffn, w_up, w_down, w_ple_gate, w_ple_proj, g_final):
    batch, seq, d = x_prompt.shape
    n_seq = x_sample.shape[0]
    depth = w_in.shape[0]
    assert d == D_MODEL and w_in.shape[2] == N_PROJ * ATT_WIDTH and x_sample.shape[1] == 1
    assert cache_k.shape[3:] == (ATT_HEADS, HEAD_W) and state_hgrn.shape[2:] == (REC_HEADS, REC_DK, REC_DK)
    n_p = batch * seq

    hp = x_prompt.reshape(n_p, d)
    hs = x_sample.reshape(n_seq, d)
    row2 = lambda a: a.reshape(1, -1)
    outs = [[] for _ in range(6)]
    for li in range(depth):
        last = li == depth - 1
        w_in_b = w_in[li].astype(BF16)
        w_out_b = w_out[li].astype(BF16)
        w_up_b = w_up[li].astype(BF16)
        w_dn_b = w_down[li].astype(BF16)
        w_g_b = w_ple_gate[li].astype(BF16)
        w_p_b = w_ple_proj[li].astype(BF16)
        lams = jnp.stack([lambda_q1[li], lambda_k1[li], lambda_q2[li], lambda_k2[li]], axis=0)
        g_sub = row2(g_subln[li])
        g_r = row2(g_rec[li])
        tail_w = (w_out_b, row2(g_ffn[li]), w_up_b, w_dn_b, w_g_b, w_p_b, row2(g_final))

        aq, ak, av, akb, avb, rq, lf, rk, rv, rg = _inproj(hp, row2(g_mix[li]), w_in_b, hgrn_lb, li, tm=1024)
        att = _attn_prompt(lams, aq, akb, avb, g_sub, li, batch, seq, tq=256, heads_per_step=2)
        rec, s_p = _hgrn_prompt(rq, rk, rv, lf, rg, g_r, batch, seq, tt=512)
        hp = _tail(hp, att, rec, p_prompt[li].reshape(n_p, PLE_DIM), *tail_w, final=last, tm=512)
        outs[0].append(ak.reshape(batch, seq, ATT_HEADS, HEAD_W))
        outs[1].append(av.reshape(batch, seq, ATT_HEADS, HEAD_W))
        outs[2].append(s_p)

        aq, ak, av, _, _, rq, lf, rk, rv, rg = _inproj(hs, row2(g_mix[li]), w_in_b, hgrn_lb, li, tm=n_seq)
        att = _attn_sample(page_table, lams, aq, ak, av, g_sub, cache_k, cache_v, li)
        rec, s_s = _hgrn_sample(rq, rk, rv, lf, rg, g_r, state_hgrn[li], group=16)
        hs = _tail(hs, att, rec, p_sample[li].reshape(n_seq, PLE_DIM), *tail_w, final=last, tm=n_seq)
        outs[3].append(ak.reshape(n_seq, 1, ATT_HEADS, HEAD_W))
        outs[4].append(av.reshape(n_seq, 1, ATT_HEADS, HEAD_W))
        outs[5].append(s_s)

    k_p, v_p, s_p, k_s, v_s, s_s = [jnp.stack(o, axis=0) for o in outs]
    return (hp.reshape(batch, seq, d), hs.reshape(n_seq, 1, d), k_p, v_p, s_p, k_s, v_s, s_s)
```

```python
import functools
import math

import jax
import jax.numpy as jnp
import numpy as np
from jax import lax
from jax.experimental import pallas as pl
from jax.experimental.pallas import tpu as pltpu

F32 = jnp.float32
BF16 = jnp.bfloat16

D_MODEL = 1024
ATT_HEADS = 4
ATT_HD = 64
HEAD_W = 2 * ATT_HD
ATT_WIDTH = ATT_HEADS * HEAD_W
REC_HEADS = 4
REC_DK = 128
REC_WIDTH = REC_HEADS * REC_DK
N_PROJ = 7
PLE_DIM = 256
D_FF = 4 * D_MODEL
EPS = 1e-6
LOG2E = math.log2(math.e)
ATT_SCALE = ATT_HD ** -0.5 * LOG2E

V7X_VMEM_BYTES = 64 * 1024 * 1024
VMEM_LIMIT = 52 * 1024 * 1024

HGRN_CHUNK = 64
HGRN_CUM_BLOCK = 256
BAND = 4
HGRN_LEVELS = (4, 8, 16, 32)
SUBLANES = 8


def _lambda_init(li):
    return 0.8 - 0.6 * math.exp(-0.3 * li)


def _rms(x, g):
    return x * lax.rsqrt(jnp.mean(x * x, axis=-1, keepdims=True) + EPS) * g


def _silu(x):
    return x * jax.nn.sigmoid(x)


def _nt_dot(a, b):
    return lax.dot_general(a, b, (((1,), (1,)), ((), ())), preferred_element_type=F32)


def _tn_dot(a, b):
    return lax.dot_general(a, b, (((0,), (0,)), ((), ())), preferred_element_type=F32)


def _lam_from(lams, li):
    a = jnp.sum(lams[0:1, :] * lams[1:2, :], axis=-1, keepdims=True)
    b = jnp.sum(lams[2:3, :] * lams[3:4, :], axis=-1, keepdims=True)
    return jnp.exp(a) - jnp.exp(b) + _lambda_init(li)


def _inproj_kernel(x_ref, g_ref, w_ref, lb_ref,
                   aq_ref, ak_ref, av_ref, akb_ref, avb_ref,
                   rq_ref, lf_ref, rk_ref, rv_ref, rg_ref, *, li):
    h = _rms(x_ref[...], g_ref[...]).astype(BF16)

    def proj(i):
        return jnp.dot(h, w_ref[:, i * ATT_WIDTH:(i + 1) * ATT_WIDTH], preferred_element_type=F32)

    aq_ref[...] = (proj(0) * ATT_SCALE).astype(BF16)
    tm = x_ref.shape[0]
    for full_ref, half_ref, val in ((ak_ref, akb_ref, proj(1)), (av_ref, avb_ref, proj(2))):
        half_ref[...] = val.astype(BF16)
        for hd in range(ATT_HEADS):
            full_ref[pl.ds(hd, tm, stride=ATT_HEADS), :] = val[:, hd * HEAD_W:(hd + 1) * HEAD_W]
    rq_ref[...] = _silu(proj(3)).astype(BF16)
    lbp = lb_ref[...]
    e = jnp.exp(lbp - jnp.max(lbp, axis=0, keepdims=True))
    lb = jnp.sum(e[:li + 1], axis=0, keepdims=True) / jnp.sum(e, axis=0, keepdims=True)
    f = lb + (1.0 - lb) * jax.nn.sigmoid(proj(4))
    lf_ref[...] = jnp.log(f)
    rk_ref[...] = (1.0 - f).astype(BF16)
    rv_ref[...] = proj(5).astype(BF16)
    rg_ref[...] = _silu(proj(6)).astype(BF16)


def _inproj(x, g_mix, w_in, hgrn_lb, li, tm):
    n = x.shape[0]
    grid = (n // tm,)
    row = lambda i: (i, 0)
    const = lambda i: (0, 0)
    wide = pl.BlockSpec((tm, ATT_WIDTH), row)
    heads = pl.BlockSpec((tm * ATT_HEADS, HEAD_W), row)
    flat = lambda dt: jax.ShapeDtypeStruct((n, ATT_WIDTH), dt)
    by_head = jax.ShapeDtypeStruct((n * ATT_HEADS, HEAD_W), F32)
    return pl.pallas_call(
        functools.partial(_inproj_kernel, li=li),
        grid=grid,
        in_specs=[pl.BlockSpec((tm, D_MODEL), row),
                  pl.BlockSpec((1, D_MODEL), const),
                  pl.BlockSpec(w_in.shape, const, pipeline_mode=pl.Buffered(1)),
                  pl.BlockSpec(hgrn_lb.shape, const)],
        out_specs=[wide, heads, heads] + [wide] * 7,
        out_shape=[flat(BF16), by_head, by_head, flat(BF16), flat(BF16),
                   flat(BF16), flat(F32), flat(BF16), flat(BF16), flat(BF16)],
        compiler_params=pltpu.CompilerParams(dimension_semantics=("arbitrary",),
                                             vmem_limit_bytes=VMEM_LIMIT),
        name="inproj",
    )(x, g_mix, w_in, hgrn_lb)


def _attn_prompt_kernel(lams_ref, q_ref, k_ref, v_ref, g_ref, o_ref, s_ref, *, li, tq):
    seq = q_ref.shape[0]
    lane_blocks = tq // HEAD_W
    lam = _lam_from(lams_ref[...], li)
    lane = lax.broadcasted_iota(jnp.int32, (tq, HEAD_W), 1)
    r = lax.broadcasted_iota(jnp.int32, (2 * tq, tq), 0)
    c = lax.broadcasted_iota(jnp.int32, (2 * tq, tq), 1)
    visible = c <= jnp.where(r >= tq, r - tq, r)

    def fold(x, op):
        out = x[:, :HEAD_W]
        for b in range(1, lane_blocks):
            out = op(out, x[:, b * HEAD_W:(b + 1) * HEAD_W])
        return out

    ring = s_ref.shape[0]

    def pass1(hh, qi, base):
        cs = slice(hh * HEAD_W, (hh + 1) * HEAD_W)
        q = q_ref[qi * tq:(qi + 1) * tq, cs]
        zero = jnp.zeros_like(q)
        qs = jnp.concatenate([jnp.where(lane < ATT_HD, q, zero),
                              jnp.where(lane >= ATT_HD, q, zero)], axis=0)
        m = None
        for j in range(qi + 1):
            s = _nt_dot(qs, k_ref[j * tq:(j + 1) * tq, cs])
            if j == qi:
                s = jnp.where(visible, s, -jnp.inf)
            s_ref[(base + j) % ring] = s
            fm = fold(s, jnp.maximum)
            m = fm if m is None else jnp.maximum(m, fm)
        return jnp.broadcast_to(jnp.max(m, axis=-1, keepdims=True), m.shape)

    def pass2(hh, qi, base, mb):
        cs = slice(hh * HEAD_W, (hh + 1) * HEAD_W)
        l = jnp.zeros((2 * tq, HEAD_W), F32)
        acc = jnp.zeros((2 * tq, HEAD_W), F32)
        for j in range(qi + 1):
            s = s_ref[(base + j) % ring]
            p = jnp.concatenate([jnp.exp2(s[:, b * HEAD_W:(b + 1) * HEAD_W] - mb)
                                 for b in range(lane_blocks)], axis=1)
            l = l + fold(p, jnp.add)
            acc = acc + jnp.dot(p.astype(BF16), v_ref[j * tq:(j + 1) * tq, cs],
                                preferred_element_type=F32)
        o = acc / jnp.sum(l, axis=-1, keepdims=True)
        d = o[:tq] - lam * o[tq:]
        o_ref[qi * tq:(qi + 1) * tq, cs] = (_rms(d, g_ref[...]) * (1.0 - _lambda_init(li))).astype(BF16)

    pending = None
    base = 0
    for qi in range(seq // tq):
        for hh in range(q_ref.shape[1] // HEAD_W):
            mb = pass1(hh, qi, base)
            if pending is not None:
                pass2(*pending)
            pending = (hh, qi, base, mb)
            base += qi + 1
    pass2(*pending)


def _attn_prompt(lams, aq, akb, avb, g_subln, li, batch, seq, tq, heads_per_step):
    nq = seq // tq
    n = batch * seq
    group = pl.BlockSpec((seq, heads_per_step * HEAD_W), lambda b, h: (b, h))
    return pl.pallas_call(
        functools.partial(_attn_prompt_kernel, li=li, tq=tq),
        grid=(batch, ATT_HEADS // heads_per_step),
        in_specs=[pl.BlockSpec(lams.shape, lambda b, h: (0, 0)),
                  group, group, group,
                  pl.BlockSpec((1, HEAD_W), lambda b, h: (0, 0))],
        out_specs=group,
        out_shape=jax.ShapeDtypeStruct((n, ATT_WIDTH), BF16),
        scratch_shapes=[pltpu.VMEM((2 * nq, 2 * tq, tq), F32)],
        compiler_params=pltpu.CompilerParams(
            dimension_semantics=("arbitrary", "arbitrary"),
            vmem_limit_bytes=VMEM_LIMIT),
        name="attn_prompt",
    )(lams, aq, akb, avb, g_subln)


SCORE_ROWS = 16


def _attn_sample_kernel(pt_ref, lams_ref, q_ref, kn_ref, vn_ref, g_ref, *rest, li, n_pages, group):
    del pt_ref
    o_ref = rest[2 * n_pages * group]
    lam = _lam_from(lams_ref[...], li)
    for s in range(group):
        k_pages = rest[s * n_pages:(s + 1) * n_pages]
        v_pages = rest[(group + s) * n_pages:(group + s + 1) * n_pages]
        o_ref[s] = _attn_one_sequence(q_ref[s].astype(F32), kn_ref[s], vn_ref[s], k_pages, v_pages,
                                      g_ref[...], lam, li)


def _attn_one_sequence(q, kn, vn, k_pages, v_pages, g, lam, li):
    n_pages = len(k_pages)
    pr = k_pages[0].shape[1]
    row = lax.broadcasted_iota(jnp.int32, (SCORE_ROWS, HEAD_W), 0)
    lane = lax.broadcasted_iota(jnp.int32, (SCORE_ROWS, HEAD_W), 1)
    qrows = jnp.zeros((SCORE_ROWS, HEAD_W), F32)
    knrows = jnp.zeros((SCORE_ROWS, HEAD_W), F32)
    vnrows = jnp.zeros((SCORE_ROWS, HEAD_W), F32)
    for h in range(ATT_HEADS):
        cs = slice(h * HEAD_W, (h + 1) * HEAD_W)
        in_head = row // 2 == h
        qrows = jnp.where(in_head & (lane // ATT_HD == row % 2), q[:, cs], qrows)
        knrows = jnp.where(in_head, kn[:, cs], knrows)
        vnrows = jnp.where(in_head, vn[:, cs], vnrows)
    qb = qrows.astype(BF16)
    s_past = jnp.concatenate([_nt_dot(qb, kp[0].astype(BF16)) for kp in k_pages], axis=1)
    srow = lax.broadcasted_iota(jnp.int32, s_past.shape, 0)
    scol = lax.broadcasted_iota(jnp.int32, s_past.shape, 1)
    own = (scol % ATT_HEADS) == (srow // 2)
    s_past = jnp.where(own, s_past, -jnp.inf)
    s_new = jnp.sum(qrows * knrows, axis=-1, keepdims=True)
    m = jnp.maximum(jnp.max(s_past, axis=-1, keepdims=True), s_new)
    p = jnp.exp2(s_past - m)
    p_new = jnp.exp2(s_new - m)
    l = jnp.sum(p, axis=-1, keepdims=True) + p_new
    acc = p_new * vnrows
    for j in range(n_pages):
        acc = acc + jnp.dot(p[:, j * pr:(j + 1) * pr].astype(BF16), v_pages[j][0].astype(BF16),
                            preferred_element_type=F32)
    o = acc / l
    outs = []
    for h in range(ATT_HEADS):
        d = o[2 * h:2 * h + 1, :] - lam * o[2 * h + 1:2 * h + 2, :]
        outs.append(_rms(d, g) * (1.0 - _lambda_init(li)))
    return jnp.concatenate(outs, axis=1).astype(BF16)


def _attn_sample(page_table, lams, aq, ak, av, g_subln, cache_k, cache_v, li, group):
    n_seq, n_pages = page_table.shape
    depth, n_pool, page = cache_k.shape[:3]
    pr = page * ATT_HEADS
    ck = cache_k.reshape(depth * n_pool, pr, HEAD_W)
    cv = cache_v.reshape(depth * n_pool, pr, HEAD_W)
    base = li * n_pool

    def page_spec(s, j):
        return pl.BlockSpec((1, pr, HEAD_W), lambda i, pt: (base + pt[i * group + s, j], 0, 0))

    pages = [page_spec(s, j) for s in range(group) for j in range(n_pages)]
    tok = pl.BlockSpec((group, 1, ATT_WIDTH), lambda i, pt: (i, 0, 0))
    grid_spec = pltpu.PrefetchScalarGridSpec(
        num_scalar_prefetch=1,
        grid=(n_seq // group,),
        in_specs=[pl.BlockSpec(lams.shape, lambda i, pt: (0, 0)), tok, tok, tok,
                  pl.BlockSpec((1, HEAD_W), lambda i, pt: (0, 0))] + pages * 2,
        out_specs=tok,
    )
    out = pl.pallas_call(
        functools.partial(_attn_sample_kernel, li=li, n_pages=n_pages, group=group),
        grid_spec=grid_spec,
        out_shape=jax.ShapeDtypeStruct((n_seq, 1, ATT_WIDTH), BF16),
        compiler_params=pltpu.CompilerParams(dimension_semantics=("arbitrary",),
                                             vmem_limit_bytes=VMEM_LIMIT),
        name="attn_sample",
    )(page_table, lams, aq.reshape(n_seq, 1, ATT_WIDTH), ak.reshape(n_seq, 1, ATT_WIDTH),
      av.reshape(n_seq, 1, ATT_WIDTH), g_subln,
      *([ck] * (n_pages * group)), *([cv] * (n_pages * group)))
    return out.reshape(n_seq, ATT_WIDTH)


def _split3(x):
    hi = x.astype(BF16)
    r = x - hi.astype(F32)
    mid = r.astype(BF16)
    lo = (r - mid.astype(F32)).astype(BF16)
    return hi, mid, lo


def _hgrn_prompt_kernel(tri_ref, q_ref, k_ref, v_ref, lf_ref, gate_ref, grec_ref,
                        o_ref, s_ref, st_ref, b_ref, acc_ref, *, tt):
    t = pl.program_id(1)
    nt = pl.num_programs(1)
    c = HGRN_CHUNK

    @pl.when(t == 0)
    def _():
        st_ref[...] = jnp.zeros_like(st_ref)

    tri = tri_ref[...]
    for r0 in range(0, tt, HGRN_CUM_BLOCK):
        lf2 = lf_ref[r0:r0 + HGRN_CUM_BLOCK, :] * LOG2E
        hi, mid, lo = _split3(lf2)
        b_ref[r0:r0 + HGRN_CUM_BLOCK, :] = (
            jnp.dot(tri, hi, preferred_element_type=F32)
            + jnp.dot(tri, mid, preferred_element_type=F32)
            + jnp.dot(tri, lo, preferred_element_type=F32))

    rowi = lax.broadcasted_iota(jnp.int32, (c, c), 0)
    coli = lax.broadcasted_iota(jnp.int32, (c, c), 1)
    pair = jnp.where(coli > rowi, -1, BAND + len(HGRN_LEVELS) - 1)
    for level in reversed(range(len(HGRN_LEVELS))):
        blk = HGRN_LEVELS[level]
        cls = rowi - coli if level == 0 else BAND + level - 1
        pair = jnp.where((rowi // blk == coli // blk) & (coli <= rowi), cls, pair)

    def chunk_body(ci, carry):
        r0 = pl.multiple_of(ci * c, c)
        heads = [slice(h * REC_DK, (h + 1) * REC_DK) for h in range(REC_HEADS)]
        qs = [q_ref[pl.ds(r0, c), cs].astype(F32) for cs in heads]
        ks = [k_ref[pl.ds(r0, c), cs].astype(F32) for cs in heads]
        bs = [b_ref[pl.ds(r0, c), cs] for cs in heads]
        o_state, cross = [], []
        for h, cs in enumerate(heads):
            q, k, b = qs[h], ks[h], bs[h]
            vb = v_ref[pl.ds(r0, c), cs]
            st = st_ref[h]
            o_state.append(_nt_dot((q * jnp.exp2(b)).astype(BF16), st.astype(BF16)))
            bl = b[c - 1:c, :]
            kd = (k * jnp.exp2(bl - b)).astype(BF16)
            st_ref[h] = st * jnp.exp2(bl) + _tn_dot(vb, kd)
            per_level = []
            for half in HGRN_LEVELS:
                blk = 2 * half
                ref = jnp.concatenate(
                    [jnp.broadcast_to(b[g0 * blk + half - 1:g0 * blk + half, :], (blk, REC_DK))
                     for g0 in range(c // blk)], axis=0)
                d = b - ref
                qe = jnp.minimum(d, 0.0)
                qf = (q * jnp.exp2(qe)).astype(BF16)
                kf = (k * jnp.exp2(qe - d)).astype(BF16)
                per_level.append(_nt_dot(qf, kf))
            cross.append(per_level)
        for h, cs in enumerate(heads):
            q = qs[h]
            f = jnp.exp(lf_ref[pl.ds(r0, c), cs])
            g = ks[h]
            a = jnp.zeros((c, c), F32)
            for dist in range(BAND):
                if dist > 0:
                    g3 = g.reshape(c // SUBLANES, SUBLANES, REC_DK)
                    g = f * pltpu.roll(g3, 1, 1).reshape(c, REC_DK)
                w = jnp.sum(q * g, axis=-1, keepdims=True)
                a = jnp.where(pair == dist, w, a)
            for level in range(len(HGRN_LEVELS)):
                a = jnp.where(pair == BAND + level, cross[h][level], a)
            vb = v_ref[pl.ds(r0, c), cs]
            acc_ref[pl.ds(r0, c), cs] = o_state[h] + jnp.dot(a.astype(BF16), vb,
                                                             preferred_element_type=F32)
        return carry

    lax.fori_loop(0, tt // c, chunk_body, 0, unroll=True)

    o_all = acc_ref[...]
    o_ref[...] = (_rms(o_all, grec_ref[...]) * gate_ref[...].astype(F32)).astype(BF16)

    @pl.when(t == nt - 1)
    def _():
        for h in range(REC_HEADS):
            s_ref[0, h] = st_ref[h].T


def _hgrn_tri():
    i = np.arange(HGRN_CUM_BLOCK)
    m = (i[:, None] >= i[None, :]) & ((i[:, None] // HGRN_CHUNK) == (i[None, :] // HGRN_CHUNK))
    return jnp.asarray(m.astype(np.float32), dtype=BF16)


def _hgrn_prompt(rq, rk, rv, lf, rg, g_rec, batch, seq, tt):
    nt = seq // tt
    n = batch * seq
    tile = pl.BlockSpec((tt, REC_WIDTH), lambda b, t: (b * nt + t, 0))
    return pl.pallas_call(
        functools.partial(_hgrn_prompt_kernel, tt=tt),
        grid=(batch, nt),
        in_specs=[pl.BlockSpec((HGRN_CUM_BLOCK, HGRN_CUM_BLOCK), lambda b, t: (0, 0)),
                  tile, tile, tile, tile, tile,
                  pl.BlockSpec((1, REC_WIDTH), lambda b, t: (0, 0))],
        out_specs=[tile, pl.BlockSpec((1, REC_HEADS, REC_DK, REC_DK), lambda b, t: (b, 0, 0, 0))],
        out_shape=[jax.ShapeDtypeStruct((n, REC_WIDTH), BF16),
                   jax.ShapeDtypeStruct((batch, REC_HEADS, REC_DK, REC_DK), F32)],
        scratch_shapes=[pltpu.VMEM((REC_HEADS, REC_DK, REC_DK), F32),
                        pltpu.VMEM((tt, REC_WIDTH), F32),
                        pltpu.VMEM((tt, REC_WIDTH), F32)],
        compiler_params=pltpu.CompilerParams(dimension_semantics=("arbitrary", "arbitrary"),
                                             vmem_limit_bytes=VMEM_LIMIT),
        name="hgrn_prompt",
    )(_hgrn_tri(), rq, rk, rv, lf, rg, g_rec)


def _hgrn_sample_kernel(q_ref, k_ref, v_ref, lf_ref, gate_ref, grec_ref, s0_ref, o_ref, s_ref, *, group):
    rowi = lax.broadcasted_iota(jnp.int32, (group, REC_DK), 0)
    outs = []
    for h in range(REC_HEADS):
        cs = slice(h * REC_DK, (h + 1) * REC_DK)
        q = q_ref[:, cs]
        k = k_ref[:, cs]
        v = v_ref[:, cs]
        ft = jnp.exp(lf_ref[:, cs]).T
        outer = []
        for n in range(group):
            kn = jnp.where(rowi == n, k, jnp.zeros_like(k))
            outer.append(_tn_dot(kn, v))
        new = []
        for n in range(group):
            st = s0_ref[n, h] * ft[:, n:n + 1] + outer[n]
            s_ref[n, h] = st
            new.append(st.astype(BF16))
        o_h = jnp.zeros((group, REC_DK), F32)
        for n in range(group):
            o_h = jnp.where(rowi == n, jnp.dot(q, new[n], preferred_element_type=F32), o_h)
        outs.append(o_h)
    o = jnp.concatenate(outs, axis=1)
    o_ref[...] = (_rms(o, grec_ref[...]) * gate_ref[...].astype(F32)).astype(BF16)


def _hgrn_sample(rq, rk, rv, lf, rg, g_rec, s0, group):
    n = rq.shape[0]
    tile = pl.BlockSpec((group, REC_WIDTH), lambda i: (i, 0))
    st = pl.BlockSpec((group, REC_HEADS, REC_DK, REC_DK), lambda i: (i, 0, 0, 0))
    return pl.pallas_call(
        functools.partial(_hgrn_sample_kernel, group=group),
        grid=(n // group,),
        in_specs=[tile, tile, tile, tile, tile, pl.BlockSpec((1, REC_WIDTH), lambda i: (0, 0)), st],
        out_specs=[tile, st],
        out_shape=[jax.ShapeDtypeStruct((n, REC_WIDTH), BF16),
                   jax.ShapeDtypeStruct(s0.shape, F32)],
        compiler_params=pltpu.CompilerParams(dimension_semantics=("arbitrary",),
                                             vmem_limit_bytes=VMEM_LIMIT),
        name="hgrn_sample",
    )(rq, rk, rv, lf, rg, g_rec, s0)


FF_BLOCK = 1024


def _tail_kernel(x_ref, att_ref, rec_ref, p_ref, wo_ref, gffn_ref, wup_ref, wdn_ref,
                 wg_ref, wp_ref, gfin_ref, y_ref, *, final):
    mix = (jnp.dot(att_ref[...], wo_ref[:ATT_WIDTH, :], preferred_element_type=F32)
           + jnp.dot(rec_ref[...], wo_ref[ATT_WIDTH:, :], preferred_element_type=F32))
    x = x_ref[...] + mix
    h = _rms(x, gffn_ref[...]).astype(BF16)
    for c0 in range(0, D_FF, FF_BLOCK):
        up = jnp.dot(h, wup_ref[:, c0:c0 + FF_BLOCK], preferred_element_type=F32)
        act = jnp.square(jnp.maximum(up, 0.0)).astype(BF16)
        x = x + jnp.dot(act, wdn_ref[c0:c0 + FF_BLOCK, :], preferred_element_type=F32)
    gate = jax.nn.sigmoid(jnp.dot(x.astype(BF16), wg_ref[...], preferred_element_type=F32))
    emb = jnp.dot(p_ref[...].astype(BF16), wp_ref[...], preferred_element_type=F32)
    x = x + gate * emb
    y_ref[...] = _rms(x, gfin_ref[...]) if final else x


def _tail(x, att, rec, p, w_out, g_ffn, w_up, w_down, w_gate, w_proj, g_final, final, tm):
    n = x.shape[0]
    row = lambda i: (i, 0)
    const = lambda i: (0, 0)

    def resident(a):
        return pl.BlockSpec(a.shape, const, pipeline_mode=pl.Buffered(1))

    return pl.pallas_call(
        functools.partial(_tail_kernel, final=final),
        grid=(n // tm,),
        in_specs=[pl.BlockSpec((tm, D_MODEL), row),
                  pl.BlockSpec((tm, ATT_WIDTH), row),
                  pl.BlockSpec((tm, REC_WIDTH), row),
                  pl.BlockSpec((tm, PLE_DIM), row),
                  resident(w_out), resident(g_ffn), resident(w_up), resident(w_down),
                  resident(w_gate), resident(w_proj), resident(g_final)],
        out_specs=pl.BlockSpec((tm, D_MODEL), row),
        out_shape=jax.ShapeDtypeStruct((n, D_MODEL), F32),
        compiler_params=pltpu.CompilerParams(dimension_semantics=("arbitrary",),
                                             vmem_limit_bytes=VMEM_LIMIT),
        name="tail",
    )(x, att, rec, p, w_out, g_ffn, w_up, w_down, w_gate, w_proj, g_final)


def kernel(x_prompt, x_sample, cache_k, cache_v, state_hgrn, page_table, p_prompt, p_sample,
           w_in, lambda_q1, lambda_k1, lambda_q2, lambda_k2, g_subln, hgrn_lb, g_rec, w_out,
           g_mix, g_ffn, w_up, w_down, w_ple_gate, w_ple_proj, g_final):
    batch, seq, d = x_prompt.shape
    n_seq = x_sample.shape[0]
    depth = w_in.shape[0]
    assert d == D_MODEL and w_in.shape[2] == N_PROJ * ATT_WIDTH and x_sample.shape[1] == 1
    assert cache_k.shape[3:] == (ATT_HEADS, HEAD_W) and state_hgrn.shape[2:] == (REC_HEADS, REC_DK, REC_DK)
    n_p = batch * seq

    hp = x_prompt.reshape(n_p, d)
    hs = x_sample.reshape(n_seq, d)
    row2 = lambda a: a.reshape(1, -1)
    outs = [[] for _ in range(6)]
    for li in range(depth):
        last = li == depth - 1
        w_in_b = w_in[li].astype(BF16)
        w_out_b = w_out[li].astype(BF16)
        w_up_b = w_up[li].astype(BF16)
        w_dn_b = w_down[li].astype(BF16)
        w_g_b = w_ple_gate[li].astype(BF16)
        w_p_b = w_ple_proj[li].astype(BF16)
        lams = jnp.stack([lambda_q1[li], lambda_k1[li], lambda_q2[li], lambda_k2[li]], axis=0)
        g_sub = row2(g_subln[li])
        g_r = row2(g_rec[li])
        tail_w = (w_out_b, row2(g_ffn[li]), w_up_b, w_dn_b, w_g_b, w_p_b, row2(g_final))

        aq, ak, av, akb, avb, rq, lf, rk, rv, rg = _inproj(hp, row2(g_mix[li]), w_in_b, hgrn_lb, li, tm=1024)
        att = _attn_prompt(lams, aq, akb, avb, g_sub, li, batch, seq, tq=256, heads_per_step=2)
        rec, s_p = _hgrn_prompt(rq, rk, rv, lf, rg, g_r, batch, seq, tt=512)
        hp = _tail(hp, att, rec, p_prompt[li].reshape(n_p, PLE_DIM), *tail_w, final=last, tm=512)
        outs[0].append(ak.reshape(batch, seq, ATT_HEADS, HEAD_W))
        outs[1].append(av.reshape(batch, seq, ATT_HEADS, HEAD_W))
        outs[2].append(s_p)

        aq, ak, av, _, _, rq, lf, rk, rv, rg = _inproj(hs, row2(g_mix[li]), w_in_b, hgrn_lb, li, tm=n_seq)
        att = _attn_sample(page_table, lams, aq, ak, av, g_sub, cache_k, cache_v, li, group=2)
        rec, s_s = _hgrn_sample(rq, rk, rv, lf, rg, g_r, state_hgrn[li], group=16)
        hs = _tail(hs, att, rec, p_sample[li].reshape(n_seq, PLE_DIM), *tail_w, final=last, tm=n_seq)
        outs[3].append(ak.reshape(n_seq, 1, ATT_HEADS, HEAD_W))
        outs[4].append(av.reshape(n_seq, 1, ATT_HEADS, HEAD_W))
        outs[5].append(s_s)

    k_p, v_p, s_p, k_s, v_s, s_s = [jnp.stack(o, axis=0) for o in outs]
    return (hp.reshape(batch, seq, d), hs.reshape(n_seq, 1, d), k_p, v_p, s_p, k_s, v_s, s_s)
```

```python
import functools
import math

import jax
import jax.numpy as jnp
import numpy as np
from jax import lax
from jax.experimental import pallas as pl
from jax.experimental.pallas import tpu as pltpu

F32 = jnp.float32
BF16 = jnp.bfloat16

D_MODEL = 1024
ATT_HEADS = 4
ATT_HD = 64
HEAD_W = 2 * ATT_HD
ATT_WIDTH = ATT_HEADS * HEAD_W
REC_HEADS = 4
REC_DK = 128
REC_WIDTH = REC_HEADS * REC_DK
N_PROJ = 7
PLE_DIM = 256
D_FF = 4 * D_MODEL
EPS = 1e-6
LOG2E = math.log2(math.e)
ATT_SCALE = ATT_HD ** -0.5 * LOG2E

V7X_VMEM_BYTES = 64 * 1024 * 1024
VMEM_LIMIT = V7X_VMEM_BYTES * 13 // 16

HGRN_CHUNK = 64
HGRN_CUM_BLOCK = 256
BAND = 4
HGRN_LEVELS = (4, 8, 16, 32)
SUBLANES = 8


def _lambda_init(li):
    return 0.8 - 0.6 * math.exp(-0.3 * li)


def _rms(x, g):
    return x * lax.rsqrt(jnp.mean(x * x, axis=-1, keepdims=True) + EPS) * g


def _silu(x):
    return x * jax.nn.sigmoid(x)


def _nt_dot(a, b):
    return lax.dot_general(a, b, (((1,), (1,)), ((), ())), preferred_element_type=F32)


def _tn_dot(a, b):
    return lax.dot_general(a, b, (((0,), (0,)), ((), ())), preferred_element_type=F32)


def _lam_from(lams, li):
    a = jnp.sum(lams[0:1, :] * lams[1:2, :], axis=-1, keepdims=True)
    b = jnp.sum(lams[2:3, :] * lams[3:4, :], axis=-1, keepdims=True)
    return jnp.exp(a) - jnp.exp(b) + _lambda_init(li)


def _inproj_kernel(x_ref, g_ref, w_ref, lb_ref,
                   aq_ref, ak_ref, av_ref, akb_ref, avb_ref,
                   rq_ref, lf_ref, rk_ref, rv_ref, rg_ref, *, li):
    h = _rms(x_ref[...], g_ref[...]).astype(BF16)

    def proj(i):
        return jnp.dot(h, w_ref[:, i * ATT_WIDTH:(i + 1) * ATT_WIDTH], preferred_element_type=F32)

    lbp = lb_ref[...]
    e = jnp.exp(lbp - jnp.max(lbp, axis=0, keepdims=True))
    lb = jnp.sum(e[:li + 1], axis=0, keepdims=True) / jnp.sum(e, axis=0, keepdims=True)
    f = lb + (1.0 - lb) * jax.nn.sigmoid(proj(4))
    lf_ref[...] = jnp.log(f)
    rk_ref[...] = (1.0 - f).astype(BF16)
    rq_ref[...] = _silu(proj(3)).astype(BF16)
    rg_ref[...] = _silu(proj(6)).astype(BF16)
    tm = x_ref.shape[0]
    for full_ref, half_ref, val in ((ak_ref, akb_ref, proj(1)), (av_ref, avb_ref, proj(2))):
        half_ref[...] = val.astype(BF16)
        for hd in range(ATT_HEADS):
            full_ref[pl.ds(hd, tm, stride=ATT_HEADS), :] = val[:, hd * HEAD_W:(hd + 1) * HEAD_W]
    aq_ref[...] = (proj(0) * ATT_SCALE).astype(BF16)
    rv_ref[...] = proj(5).astype(BF16)


def _inproj(x, g_mix, w_in, hgrn_lb, li, tm):
    n = x.shape[0]
    grid = (n // tm,)
    row = lambda i: (i, 0)
    const = lambda i: (0, 0)
    wide = pl.BlockSpec((tm, ATT_WIDTH), row)
    heads = pl.BlockSpec((tm * ATT_HEADS, HEAD_W), row)
    flat = lambda dt: jax.ShapeDtypeStruct((n, ATT_WIDTH), dt)
    by_head = jax.ShapeDtypeStruct((n * ATT_HEADS, HEAD_W), F32)
    return pl.pallas_call(
        functools.partial(_inproj_kernel, li=li),
        grid=grid,
        in_specs=[pl.BlockSpec((tm, D_MODEL), row),
                  pl.BlockSpec((1, D_MODEL), const),
                  pl.BlockSpec(w_in.shape, const, pipeline_mode=pl.Buffered(1)),
                  pl.BlockSpec(hgrn_lb.shape, const)],
        out_specs=[wide, heads, heads] + [wide] * 7,
        out_shape=[flat(BF16), by_head, by_head, flat(BF16), flat(BF16),
                   flat(BF16), flat(F32), flat(BF16), flat(BF16), flat(BF16)],
        compiler_params=pltpu.CompilerParams(dimension_semantics=("arbitrary",),
                                             vmem_limit_bytes=VMEM_LIMIT),
        name="inproj",
    )(x, g_mix, w_in, hgrn_lb)


def _attn_prompt_kernel(lams_ref, q_ref, k_ref, v_ref, g_ref, o_ref, s_ref, *, li, tq):
    seq = q_ref.shape[0]
    lane_blocks = tq // HEAD_W
    lam = _lam_from(lams_ref[...], li)
    lane = lax.broadcasted_iota(jnp.int32, (tq, HEAD_W), 1)
    r = lax.broadcasted_iota(jnp.int32, (2 * tq, tq), 0)
    c = lax.broadcasted_iota(jnp.int32, (2 * tq, tq), 1)
    visible = c <= jnp.where(r >= tq, r - tq, r)

    def fold(x, op):
        out = x[:, :HEAD_W]
        for b in range(1, lane_blocks):
            out = op(out, x[:, b * HEAD_W:(b + 1) * HEAD_W])
        return out

    ring = s_ref.shape[0]

    def pass1(hh, qi, base):
        cs = slice(hh * HEAD_W, (hh + 1) * HEAD_W)
        q = q_ref[qi * tq:(qi + 1) * tq, cs]
        zero = jnp.zeros_like(q)
        qs = jnp.concatenate([jnp.where(lane < ATT_HD, q, zero),
                              jnp.where(lane >= ATT_HD, q, zero)], axis=0)
        m = None
        for j in range(qi + 1):
            s = _nt_dot(qs, k_ref[j * tq:(j + 1) * tq, cs])
            if j == qi:
                s = jnp.where(visible, s, -jnp.inf)
            s_ref[(base + j) % ring] = s
            fm = fold(s, jnp.maximum)
            m = fm if m is None else jnp.maximum(m, fm)
        return jnp.broadcast_to(jnp.max(m, axis=-1, keepdims=True), m.shape)

    def pass2(hh, qi, base, mb):
        cs = slice(hh * HEAD_W, (hh + 1) * HEAD_W)
        l = jnp.zeros((2 * tq, HEAD_W), F32)
        acc = jnp.zeros((2 * tq, HEAD_W), F32)
        for j in range(qi + 1):
            s = s_ref[(base + j) % ring]
            p = jnp.concatenate([jnp.exp2(s[:, b * HEAD_W:(b + 1) * HEAD_W] - mb)
                                 for b in range(lane_blocks)], axis=1)
            l = l + fold(p, jnp.add)
            acc = acc + jnp.dot(p.astype(BF16), v_ref[j * tq:(j + 1) * tq, cs],
                                preferred_element_type=F32)
        o = acc / jnp.sum(l, axis=-1, keepdims=True)
        d = o[:tq] - lam * o[tq:]
        o_ref[qi * tq:(qi + 1) * tq, cs] = (_rms(d, g_ref[...]) * (1.0 - _lambda_init(li))).astype(BF16)

    pending = None
    base = 0
    for qi in range(seq // tq):
        for hh in range(q_ref.shape[1] // HEAD_W):
            mb = pass1(hh, qi, base)
            if pending is not None:
                pass2(*pending)
            pending = (hh, qi, base, mb)
            base += qi + 1
    pass2(*pending)


def _attn_prompt(lams, aq, akb, avb, g_subln, li, batch, seq, tq, heads_per_step):
    nq = seq // tq
    n = batch * seq
    group = pl.BlockSpec((seq, heads_per_step * HEAD_W), lambda b, h: (b, h))
    return pl.pallas_call(
        functools.partial(_attn_prompt_kernel, li=li, tq=tq),
        grid=(batch, ATT_HEADS // heads_per_step),
        in_specs=[pl.BlockSpec(lams.shape, lambda b, h: (0, 0)),
                  group, group, group,
                  pl.BlockSpec((1, HEAD_W), lambda b, h: (0, 0))],
        out_specs=group,
        out_shape=jax.ShapeDtypeStruct((n, ATT_WIDTH), BF16),
        scratch_shapes=[pltpu.VMEM((2 * nq, 2 * tq, tq), F32)],
        compiler_params=pltpu.CompilerParams(
            dimension_semantics=("arbitrary", "arbitrary"),
            vmem_limit_bytes=VMEM_LIMIT),
        name="attn_prompt",
    )(lams, aq, akb, avb, g_subln)


SCORE_ROWS = 16


def _attn_sample_kernel(pt_ref, lams_ref, q_ref, kn_ref, vn_ref, g_ref, *rest, li, n_pages, group):
    del pt_ref
    o_ref = rest[2 * n_pages * group]
    lam = _lam_from(lams_ref[...], li)
    for s in range(group):
        k_pages = rest[s * n_pages:(s + 1) * n_pages]
        v_pages = rest[(group + s) * n_pages:(group + s + 1) * n_pages]
        o_ref[s] = _attn_one_sequence(q_ref[s].astype(F32), kn_ref[s], vn_ref[s], k_pages, v_pages,
                                      g_ref[...], lam, li)


def _attn_one_sequence(q, kn, vn, k_pages, v_pages, g, lam, li):
    n_pages = len(k_pages)
    pr = k_pages[0].shape[1]
    row = lax.broadcasted_iota(jnp.int32, (SCORE_ROWS, HEAD_W), 0)
    lane = lax.broadcasted_iota(jnp.int32, (SCORE_ROWS, HEAD_W), 1)
    qrows = jnp.zeros((SCORE_ROWS, HEAD_W), F32)
    knrows = jnp.zeros((SCORE_ROWS, HEAD_W), F32)
    vnrows = jnp.zeros((SCORE_ROWS, HEAD_W), F32)
    for h in range(ATT_HEADS):
        cs = slice(h * HEAD_W, (h + 1) * HEAD_W)
        in_head = row // 2 == h
        qrows = jnp.where(in_head & (lane // ATT_HD == row % 2), q[:, cs], qrows)
        knrows = jnp.where(in_head, kn[:, cs], knrows)
        vnrows = jnp.where(in_head, vn[:, cs], vnrows)
    qb = qrows.astype(BF16)
    s_past = jnp.concatenate([_nt_dot(qb, kp[0].astype(BF16)) for kp in k_pages], axis=1)
    srow = lax.broadcasted_iota(jnp.int32, s_past.shape, 0)
    scol = lax.broadcasted_iota(jnp.int32, s_past.shape, 1)
    own = (scol % ATT_HEADS) == (srow // 2)
    s_past = jnp.where(own, s_past, -jnp.inf)
    s_new = jnp.sum(qrows * knrows, axis=-1, keepdims=True)
    m = jnp.maximum(jnp.max(s_past, axis=-1, keepdims=True), s_new)
    p = jnp.exp2(s_past - m)
    p_new = jnp.exp2(s_new - m)
    l = jnp.sum(p, axis=-1, keepdims=True) + p_new
    acc = p_new * vnrows
    for j in range(n_pages):
        acc = acc + jnp.dot(p[:, j * pr:(j + 1) * pr].astype(BF16), v_pages[j][0].astype(BF16),
                            preferred_element_type=F32)
    o = acc / l
    outs = []
    for h in range(ATT_HEADS):
        d = o[2 * h:2 * h + 1, :] - lam * o[2 * h + 1:2 * h + 2, :]
        outs.append(_rms(d, g) * (1.0 - _lambda_init(li)))
    return jnp.concatenate(outs, axis=1).astype(BF16)


def _attn_sample(page_table, lams, aq, ak, av, g_subln, cache_k, cache_v, li, group):
    n_seq, n_pages = page_table.shape
    depth, n_pool, page = cache_k.shape[:3]
    pr = page * ATT_HEADS
    ck = cache_k.reshape(depth * n_pool, pr, HEAD_W)
    cv = cache_v.reshape(depth * n_pool, pr, HEAD_W)
    base = li * n_pool

    def page_spec(s, j):
        return pl.BlockSpec((1, pr, HEAD_W), lambda i, pt: (base + pt[i * group + s, j], 0, 0))

    pages = [page_spec(s, j) for s in range(group) for j in range(n_pages)]
    tok = pl.BlockSpec((group, 1, ATT_WIDTH), lambda i, pt: (i, 0, 0))
    grid_spec = pltpu.PrefetchScalarGridSpec(
        num_scalar_prefetch=1,
        grid=(n_seq // group,),
        in_specs=[pl.BlockSpec(lams.shape, lambda i, pt: (0, 0)), tok, tok, tok,
                  pl.BlockSpec((1, HEAD_W), lambda i, pt: (0, 0))] + pages * 2,
        out_specs=tok,
    )
    out = pl.pallas_call(
        functools.partial(_attn_sample_kernel, li=li, n_pages=n_pages, group=group),
        grid_spec=grid_spec,
        out_shape=jax.ShapeDtypeStruct((n_seq, 1, ATT_WIDTH), BF16),
        compiler_params=pltpu.CompilerParams(dimension_semantics=("arbitrary",),
                                             vmem_limit_bytes=VMEM_LIMIT),
        name="attn_sample",
    )(page_table, lams, aq.reshape(n_seq, 1, ATT_WIDTH), ak.reshape(n_seq, 1, ATT_WIDTH),
      av.reshape(n_seq, 1, ATT_WIDTH), g_subln,
      *([ck] * (n_pages * group)), *([cv] * (n_pages * group)))
    return out.reshape(n_seq, ATT_WIDTH)


def _split3(x):
    hi = x.astype(BF16)
    r = x - hi.astype(F32)
    mid = r.astype(BF16)
    lo = (r - mid.astype(F32)).astype(BF16)
    return hi, mid, lo


def _hgrn_prompt_kernel(tri_ref, q_ref, k_ref, v_ref, lf_ref, gate_ref, grec_ref,
                        o_ref, s_ref, st_ref, b_ref, acc_ref, *, tt):
    t = pl.program_id(1)
    nt = pl.num_programs(1)
    c = HGRN_CHUNK

    @pl.when(t == 0)
    def _():
        st_ref[...] = jnp.zeros_like(st_ref)

    tri = tri_ref[...]
    for r0 in range(0, tt, HGRN_CUM_BLOCK):
        lf2 = lf_ref[r0:r0 + HGRN_CUM_BLOCK, :] * LOG2E
        hi, mid, lo = _split3(lf2)
        b_ref[r0:r0 + HGRN_CUM_BLOCK, :] = (
            jnp.dot(tri, hi, preferred_element_type=F32)
            + jnp.dot(tri, mid, preferred_element_type=F32)
            + jnp.dot(tri, lo, preferred_element_type=F32))

    rowi = lax.broadcasted_iota(jnp.int32, (c, c), 0)
    coli = lax.broadcasted_iota(jnp.int32, (c, c), 1)
    pair = jnp.where(coli > rowi, -1, BAND + len(HGRN_LEVELS) - 1)
    for level in reversed(range(len(HGRN_LEVELS))):
        blk = HGRN_LEVELS[level]
        cls = rowi - coli if level == 0 else BAND + level - 1
        pair = jnp.where((rowi // blk == coli // blk) & (coli <= rowi), cls, pair)

    def chunk_body(ci, carry):
        r0 = pl.multiple_of(ci * c, c)
        heads = [slice(h * REC_DK, (h + 1) * REC_DK) for h in range(REC_HEADS)]
        qs = [q_ref[pl.ds(r0, c), cs].astype(F32) for cs in heads]
        ks = [k_ref[pl.ds(r0, c), cs].astype(F32) for cs in heads]
        bs = [b_ref[pl.ds(r0, c), cs] for cs in heads]
        o_state, cross = [], []
        for h, cs in enumerate(heads):
            q, k, b = qs[h], ks[h], bs[h]
            vb = v_ref[pl.ds(r0, c), cs]
            st = st_ref[h]
            o_state.append(_nt_dot((q * jnp.exp2(b)).astype(BF16), st.astype(BF16)))
            bl = b[c - 1:c, :]
            kd = (k * jnp.exp2(bl - b)).astype(BF16)
            st_ref[h] = st * jnp.exp2(bl) + _tn_dot(vb, kd)
            per_level = []
            for half in HGRN_LEVELS:
                blk = 2 * half
                if half % SUBLANES:
                    ref = jnp.concatenate(
                        [jnp.broadcast_to(b[g0 * blk + half - 1:g0 * blk + half, :], (blk, REC_DK))
                         for g0 in range(c // blk)], axis=0)
                    d = b - ref
                    qe = jnp.minimum(d, 0.0)
                    qf = q * jnp.exp2(qe)
                    kf = k * jnp.exp2(qe - d)
                else:
                    q_rows, k_rows = [], []
                    for g0 in range(c // blk):
                        lo, mid, hi = g0 * blk, g0 * blk + half, (g0 + 1) * blk
                        ref = b[mid - 1:mid, :]
                        k_rows += [k[lo:mid] * jnp.exp2(ref - b[lo:mid]), k[mid:hi]]
                        q_rows += [q[lo:mid], q[mid:hi] * jnp.exp2(b[mid:hi] - ref)]
                    qf = jnp.concatenate(q_rows, axis=0)
                    kf = jnp.concatenate(k_rows, axis=0)
                per_level.append(_nt_dot(qf.astype(BF16), kf.astype(BF16)))
            cross.append(per_level)
        for h, cs in enumerate(heads):
            q = qs[h]
            f = jnp.exp(lf_ref[pl.ds(r0, c), cs])
            g = ks[h]
            a = jnp.zeros((c, c), F32)
            for dist in range(BAND):
                if dist > 0:
                    g3 = g.reshape(c // SUBLANES, SUBLANES, REC_DK)
                    g = f * pltpu.roll(g3, 1, 1).reshape(c, REC_DK)
                w = jnp.sum(q * g, axis=-1, keepdims=True)
                a = jnp.where(pair == dist, w, a)
            for level in range(len(HGRN_LEVELS)):
                a = jnp.where(pair == BAND + level, cross[h][level], a)
            vb = v_ref[pl.ds(r0, c), cs]
            acc_ref[pl.ds(r0, c), cs] = o_state[h] + jnp.dot(a.astype(BF16), vb,
                                                             preferred_element_type=F32)
        return carry

    lax.fori_loop(0, tt // c, chunk_body, 0, unroll=True)

    o_all = acc_ref[...]
    o_ref[...] = (_rms(o_all, grec_ref[...]) * gate_ref[...].astype(F32)).astype(BF16)

    @pl.when(t == nt - 1)
    def _():
        for h in range(REC_HEADS):
            s_ref[0, h] = st_ref[h].T


def _hgrn_tri():
    i = np.arange(HGRN_CUM_BLOCK)
    m = (i[:, None] >= i[None, :]) & ((i[:, None] // HGRN_CHUNK) == (i[None, :] // HGRN_CHUNK))
    return jnp.asarray(m.astype(np.float32), dtype=BF16)


def _hgrn_prompt(rq, rk, rv, lf, rg, g_rec, batch, seq, tt):
    nt = seq // tt
    n = batch * seq
    tile = pl.BlockSpec((tt, REC_WIDTH), lambda b, t: (b * nt + t, 0))
    return pl.pallas_call(
        functools.partial(_hgrn_prompt_kernel, tt=tt),
        grid=(batch, nt),
        in_specs=[pl.BlockSpec((HGRN_CUM_BLOCK, HGRN_CUM_BLOCK), lambda b, t: (0, 0)),
                  tile, tile, tile, tile, tile,
                  pl.BlockSpec((1, REC_WIDTH), lambda b, t: (0, 0))],
        out_specs=[tile, pl.BlockSpec((1, REC_HEADS, REC_DK, REC_DK), lambda b, t: (b, 0, 0, 0))],
        out_shape=[jax.ShapeDtypeStruct((n, REC_WIDTH), BF16),
                   jax.ShapeDtypeStruct((batch, REC_HEADS, REC_DK, REC_DK), F32)],
        scratch_shapes=[pltpu.VMEM((REC_HEADS, REC_DK, REC_DK), F32),
                        pltpu.VMEM((tt, REC_WIDTH), F32),
                        pltpu.VMEM((tt, REC_WIDTH), F32)],
        compiler_params=pltpu.CompilerParams(dimension_semantics=("arbitrary", "arbitrary"),
                                             vmem_limit_bytes=VMEM_LIMIT),
        name="hgrn_prompt",
    )(_hgrn_tri(), rq, rk, rv, lf, rg, g_rec)


def _hgrn_sample_kernel(q_ref, k_ref, v_ref, lf_ref, gate_ref, grec_ref, s0_ref, o_ref, s_ref, *, group):
    rowi = lax.broadcasted_iota(jnp.int32, (group, REC_DK), 0)
    outs = []
    for h in range(REC_HEADS):
        cs = slice(h * REC_DK, (h + 1) * REC_DK)
        q = q_ref[:, cs]
        k = k_ref[:, cs]
        v = v_ref[:, cs]
        ft = jnp.exp(lf_ref[:, cs]).T
        outer = []
        for n in range(group):
            kn = jnp.where(rowi == n, k, jnp.zeros_like(k))
            outer.append(_tn_dot(kn, v))
        new = []
        for n in range(group):
            st = s0_ref[n, h] * ft[:, n:n + 1] + outer[n]
            s_ref[n, h] = st
            new.append(st.astype(BF16))
        o_h = jnp.zeros((group, REC_DK), F32)
        for n in range(group):
            o_h = jnp.where(rowi == n, jnp.dot(q, new[n], preferred_element_type=F32), o_h)
        outs.append(o_h)
    o = jnp.concatenate(outs, axis=1)
    o_ref[...] = (_rms(o, grec_ref[...]) * gate_ref[...].astype(F32)).astype(BF16)


def _hgrn_sample(rq, rk, rv, lf, rg, g_rec, s0, group):
    n = rq.shape[0]
    tile = pl.BlockSpec((group, REC_WIDTH), lambda i: (i, 0))
    st = pl.BlockSpec((group, REC_HEADS, REC_DK, REC_DK), lambda i: (i, 0, 0, 0))
    return pl.pallas_call(
        functools.partial(_hgrn_sample_kernel, group=group),
        grid=(n // group,),
        in_specs=[tile, tile, tile, tile, tile, pl.BlockSpec((1, REC_WIDTH), lambda i: (0, 0)), st],
        out_specs=[tile, st],
        out_shape=[jax.ShapeDtypeStruct((n, REC_WIDTH), BF16),
                   jax.ShapeDtypeStruct(s0.shape, F32)],
        compiler_params=pltpu.CompilerParams(dimension_semantics=("arbitrary",),
                                             vmem_limit_bytes=VMEM_LIMIT),
        name="hgrn_sample",
    )(rq, rk, rv, lf, rg, g_rec, s0)


FF_BLOCK = 1024
TAIL_VMEM_LIMIT = V7X_VMEM_BYTES * 7 // 8


def _tail_kernel(x_ref, att_ref, rec_ref, p_ref, wo_ref, gffn_ref, wup_ref, wdn_ref,
                 wg_ref, wp_ref, gfin_ref, y_ref, *, final):
    mix = (jnp.dot(att_ref[...], wo_ref[:ATT_WIDTH, :], preferred_element_type=F32)
           + jnp.dot(rec_ref[...], wo_ref[ATT_WIDTH:, :], preferred_element_type=F32))
    x = x_ref[...] + mix
    h = _rms(x, gffn_ref[...]).astype(BF16)
    for c0 in range(0, D_FF, FF_BLOCK):
        up = jnp.dot(h, wup_ref[:, c0:c0 + FF_BLOCK], preferred_element_type=F32)
        act = jnp.square(jnp.maximum(up, 0.0)).astype(BF16)
        x = x + jnp.dot(act, wdn_ref[c0:c0 + FF_BLOCK, :], preferred_element_type=F32)
    gate = jax.nn.sigmoid(jnp.dot(x.astype(BF16), wg_ref[...], preferred_element_type=F32))
    emb = jnp.dot(p_ref[...].astype(BF16), wp_ref[...], preferred_element_type=F32)
    x = x + gate * emb
    y_ref[...] = _rms(x, gfin_ref[...]) if final else x


def _tail(x, att, rec, p, w_out, g_ffn, w_up, w_down, w_gate, w_proj, g_final, final, tm):
    n = x.shape[0]
    row = lambda i: (i, 0)
    const = lambda i: (0, 0)

    def resident(a):
        return pl.BlockSpec(a.shape, const, pipeline_mode=pl.Buffered(1))

    return pl.pallas_call(
        functools.partial(_tail_kernel, final=final),
        grid=(n // tm,),
        in_specs=[pl.BlockSpec((tm, D_MODEL), row),
                  pl.BlockSpec((tm, ATT_WIDTH), row),
                  pl.BlockSpec((tm, REC_WIDTH), row),
                  pl.BlockSpec((tm, PLE_DIM), row),
                  resident(w_out), resident(g_ffn), resident(w_up), resident(w_down),
                  resident(w_gate), resident(w_proj), resident(g_final)],
        out_specs=pl.BlockSpec((tm, D_MODEL), row),
        out_shape=jax.ShapeDtypeStruct((n, D_MODEL), F32),
        compiler_params=pltpu.CompilerParams(dimension_semantics=("arbitrary",),
                                             vmem_limit_bytes=TAIL_VMEM_LIMIT),
        name="tail",
    )(x, att, rec, p, w_out, g_ffn, w_up, w_down, w_gate, w_proj, g_final)


def kernel(x_prompt, x_sample, cache_k, cache_v, state_hgrn, page_table, p_prompt, p_sample,
           w_in, lambda_q1, lambda_k1, lambda_q2, lambda_k2, g_subln, hgrn_lb, g_rec, w_out,
           g_mix, g_ffn, w_up, w_down, w_ple_gate, w_ple_proj, g_final):
    batch, seq, d = x_prompt.shape
    n_seq = x_sample.shape[0]
    depth = w_in.shape[0]
    assert d == D_MODEL and w_in.shape[2] == N_PROJ * ATT_WIDTH and x_sample.shape[1] == 1
    assert cache_k.shape[3:] == (ATT_HEADS, HEAD_W) and state_hgrn.shape[2:] == (REC_HEADS, REC_DK, REC_DK)
    n_p = batch * seq

    hp = x_prompt.reshape(n_p, d)
    hs = x_sample.reshape(n_seq, d)
    row2 = lambda a: a.reshape(1, -1)
    outs = [[] for _ in range(6)]
    for li in range(depth):
        last = li == depth - 1
        w_in_b = w_in[li].astype(BF16)
        w_out_b = w_out[li].astype(BF16)
        w_up_b = w_up[li].astype(BF16)
        w_dn_b = w_down[li].astype(BF16)
        w_g_b = w_ple_gate[li].astype(BF16)
        w_p_b = w_ple_proj[li].astype(BF16)
        lams = jnp.stack([lambda_q1[li], lambda_k1[li], lambda_q2[li], lambda_k2[li]], axis=0)
        g_sub = row2(g_subln[li])
        g_r = row2(g_rec[li])
        tail_w = (w_out_b, row2(g_ffn[li]), w_up_b, w_dn_b, w_g_b, w_p_b, row2(g_final))

        aq, ak, av, akb, avb, rq, lf, rk, rv, rg = _inproj(hp, row2(g_mix[li]), w_in_b, hgrn_lb, li, tm=1024)
        att = _attn_prompt(lams, aq, akb, avb, g_sub, li, batch, seq, tq=256, heads_per_step=2)
        rec, s_p = _hgrn_prompt(rq, rk, rv, lf, rg, g_r, batch, seq, tt=512)
        hp = _tail(hp, att, rec, p_prompt[li].reshape(n_p, PLE_DIM), *tail_w, final=last, tm=1024)
        outs[0].append(ak.reshape(batch, seq, ATT_HEADS, HEAD_W))
        outs[1].append(av.reshape(batch, seq, ATT_HEADS, HEAD_W))
        outs[2].append(s_p)

        aq, ak, av, _, _, rq, lf, rk, rv, rg = _inproj(hs, row2(g_mix[li]), w_in_b, hgrn_lb, li, tm=n_seq)
        att = _attn_sample(page_table, lams, aq, ak, av, g_sub, cache_k, cache_v, li, group=2)
        rec, s_s = _hgrn_sample(rq, rk, rv, lf, rg, g_r, state_hgrn[li], group=16)
        hs = _tail(hs, att, rec, p_sample[li].reshape(n_seq, PLE_DIM), *tail_w, final=last, tm=n_seq)
        outs[3].append(ak.reshape(n_seq, 1, ATT_HEADS, HEAD_W))
        outs[4].append(av.reshape(n_seq, 1, ATT_HEADS, HEAD_W))
        outs[5].append(s_s)

    k_p, v_p, s_p, k_s, v_s, s_s = [jnp.stack(o, axis=0) for o in outs]
    return (hp.reshape(batch, seq, d), hs.reshape(n_seq, 1, d), k_p, v_p, s_p, k_s, v_s, s_s)
```

```python
import functools
import math

import jax
import jax.numpy as jnp
import numpy as np
from jax import lax
from jax.experimental import pallas as pl
from jax.experimental.pallas import tpu as pltpu

F32 = jnp.float32
BF16 = jnp.bfloat16

D_MODEL = 1024
ATT_HEADS = 4
ATT_HD = 64
HEAD_W = 2 * ATT_HD
ATT_WIDTH = ATT_HEADS * HEAD_W
REC_HEADS = 4
REC_DK = 128
REC_WIDTH = REC_HEADS * REC_DK
N_PROJ = 7
PLE_DIM = 256
D_FF = 4 * D_MODEL
EPS = 1e-6
LOG2E = math.log2(math.e)
ATT_SCALE = ATT_HD ** -0.5 * LOG2E

V7X_VMEM_BYTES = 64 * 1024 * 1024
VMEM_LIMIT = V7X_VMEM_BYTES * 13 // 16

HGRN_CHUNK = 64
HGRN_CUM_BLOCK = 256
BAND = 4
HGRN_LEVELS = (4, 8, 16, 32)
SUBLANES = 8


def _lambda_init(li):
    return 0.8 - 0.6 * math.exp(-0.3 * li)


def _rms(x, g):
    return x * lax.rsqrt(jnp.mean(x * x, axis=-1, keepdims=True) + EPS) * g


def _silu(x):
    return x * jax.nn.sigmoid(x)


def _nt_dot(a, b):
    return lax.dot_general(a, b, (((1,), (1,)), ((), ())), preferred_element_type=F32)


def _tn_dot(a, b):
    return lax.dot_general(a, b, (((0,), (0,)), ((), ())), preferred_element_type=F32)


def _lam_from(lams, li):
    a = jnp.sum(lams[0:1, :] * lams[1:2, :], axis=-1, keepdims=True)
    b = jnp.sum(lams[2:3, :] * lams[3:4, :], axis=-1, keepdims=True)
    return jnp.exp(a) - jnp.exp(b) + _lambda_init(li)


def _inproj_kernel(x_ref, g_ref, w_ref, lb_ref,
                   aq_ref, ak_ref, av_ref, akb_ref, avb_ref,
                   rq_ref, lf_ref, rk_ref, rv_ref, rg_ref, *, li):
    h = _rms(x_ref[...], g_ref[...]).astype(BF16)

    def proj(i):
        return jnp.dot(h, w_ref[:, i * ATT_WIDTH:(i + 1) * ATT_WIDTH], preferred_element_type=F32)

    lbp = lb_ref[...]
    e = jnp.exp(lbp - jnp.max(lbp, axis=0, keepdims=True))
    lb = jnp.sum(e[:li + 1], axis=0, keepdims=True) / jnp.sum(e, axis=0, keepdims=True)
    f = lb + (1.0 - lb) * jax.nn.sigmoid(proj(4))
    lf_ref[...] = jnp.log(f)
    rk_ref[...] = (1.0 - f).astype(BF16)
    rq_ref[...] = _silu(proj(3)).astype(BF16)
    rg_ref[...] = _silu(proj(6)).astype(BF16)
    tm = x_ref.shape[0]
    for full_ref, half_ref, val in ((ak_ref, akb_ref, proj(1)), (av_ref, avb_ref, proj(2))):
        half_ref[...] = val.astype(BF16)
        for hd in range(ATT_HEADS):
            full_ref[pl.ds(hd, tm, stride=ATT_HEADS), :] = val[:, hd * HEAD_W:(hd + 1) * HEAD_W]
    aq_ref[...] = (proj(0) * ATT_SCALE).astype(BF16)
    rv_ref[...] = proj(5).astype(BF16)


def _inproj(x, g_mix, w_in, hgrn_lb, li, tm):
    n = x.shape[0]
    grid = (n // tm,)
    row = lambda i: (i, 0)
    const = lambda i: (0, 0)
    wide = pl.BlockSpec((tm, ATT_WIDTH), row)
    heads = pl.BlockSpec((tm * ATT_HEADS, HEAD_W), row)
    flat = lambda dt: jax.ShapeDtypeStruct((n, ATT_WIDTH), dt)
    by_head = jax.ShapeDtypeStruct((n * ATT_HEADS, HEAD_W), F32)
    return pl.pallas_call(
        functools.partial(_inproj_kernel, li=li),
        grid=grid,
        in_specs=[pl.BlockSpec((tm, D_MODEL), row),
                  pl.BlockSpec((1, D_MODEL), const),
                  pl.BlockSpec(w_in.shape, const, pipeline_mode=pl.Buffered(1)),
                  pl.BlockSpec(hgrn_lb.shape, const)],
        out_specs=[wide, heads, heads] + [wide] * 7,
        out_shape=[flat(BF16), by_head, by_head, flat(BF16), flat(BF16),
                   flat(BF16), flat(F32), flat(BF16), flat(BF16), flat(BF16)],
        compiler_params=pltpu.CompilerParams(dimension_semantics=("arbitrary",),
                                             vmem_limit_bytes=VMEM_LIMIT),
        name="inproj",
    )(x, g_mix, w_in, hgrn_lb)


def _attn_prompt_kernel(lams_ref, q_ref, k_ref, v_ref, g_ref, o_ref, s_ref, *, li, tq):
    seq = q_ref.shape[0]
    lane_blocks = tq // HEAD_W
    lam = _lam_from(lams_ref[...], li)
    lane = lax.broadcasted_iota(jnp.int32, (tq, HEAD_W), 1)
    r = lax.broadcasted_iota(jnp.int32, (2 * tq, tq), 0)
    c = lax.broadcasted_iota(jnp.int32, (2 * tq, tq), 1)
    visible = c <= jnp.where(r >= tq, r - tq, r)

    def fold(x, op):
        out = x[:, :HEAD_W]
        for b in range(1, lane_blocks):
            out = op(out, x[:, b * HEAD_W:(b + 1) * HEAD_W])
        return out

    ring = s_ref.shape[0]

    def pass1(hh, qi, base):
        cs = slice(hh * HEAD_W, (hh + 1) * HEAD_W)
        q = q_ref[qi * tq:(qi + 1) * tq, cs]
        zero = jnp.zeros_like(q)
        qs = jnp.concatenate([jnp.where(lane < ATT_HD, q, zero),
                              jnp.where(lane >= ATT_HD, q, zero)], axis=0)
        m = None
        for j in range(qi + 1):
            s = _nt_dot(qs, k_ref[j * tq:(j + 1) * tq, cs])
            if j == qi:
                s = jnp.where(visible, s, -jnp.inf)
            s_ref[(base + j) % ring] = s
            fm = fold(s, jnp.maximum)
            m = fm if m is None else jnp.maximum(m, fm)
        return jnp.broadcast_to(jnp.max(m, axis=-1, keepdims=True), m.shape)

    def pass2(hh, qi, base, mb):
        cs = slice(hh * HEAD_W, (hh + 1) * HEAD_W)
        l = jnp.zeros((2 * tq, HEAD_W), F32)
        acc = jnp.zeros((2 * tq, HEAD_W), F32)
        for j in range(qi + 1):
            s = s_ref[(base + j) % ring]
            p = jnp.concatenate([jnp.exp2(s[:, b * HEAD_W:(b + 1) * HEAD_W] - mb)
                                 for b in range(lane_blocks)], axis=1)
            l = l + fold(p, jnp.add)
            acc = acc + jnp.dot(p.astype(BF16), v_ref[j * tq:(j + 1) * tq, cs],
                                preferred_element_type=F32)
        o = acc / jnp.sum(l, axis=-1, keepdims=True)
        d = o[:tq] - lam * o[tq:]
        o_ref[qi * tq:(qi + 1) * tq, cs] = (_rms(d, g_ref[...]) * (1.0 - _lambda_init(li))).astype(BF16)

    pending = None
    base = 0
    for qi in range(seq // tq):
        for hh in range(q_ref.shape[1] // HEAD_W):
            mb = pass1(hh, qi, base)
            if pending is not None:
                pass2(*pending)
            pending = (hh, qi, base, mb)
            base += qi + 1
    pass2(*pending)


def _attn_prompt(lams, aq, akb, avb, g_subln, li, batch, seq, tq, heads_per_step):
    nq = seq // tq
    n = batch * seq
    group = pl.BlockSpec((seq, heads_per_step * HEAD_W), lambda b, h: (b, h))
    return pl.pallas_call(
        functools.partial(_attn_prompt_kernel, li=li, tq=tq),
        grid=(batch, ATT_HEADS // heads_per_step),
        in_specs=[pl.BlockSpec(lams.shape, lambda b, h: (0, 0)),
                  group, group, group,
                  pl.BlockSpec((1, HEAD_W), lambda b, h: (0, 0))],
        out_specs=group,
        out_shape=jax.ShapeDtypeStruct((n, ATT_WIDTH), BF16),
        scratch_shapes=[pltpu.VMEM((2 * nq, 2 * tq, tq), F32)],
        compiler_params=pltpu.CompilerParams(
            dimension_semantics=("arbitrary", "arbitrary"),
            vmem_limit_bytes=VMEM_LIMIT),
        name="attn_prompt",
    )(lams, aq, akb, avb, g_subln)


SCORE_ROWS = 16


def _attn_sample_kernel(pt_ref, lams_ref, q_ref, kn_ref, vn_ref, g_ref, *rest, li, n_pages, group):
    del pt_ref
    o_ref = rest[2 * n_pages * group]
    lam = _lam_from(lams_ref[...], li)
    for s in range(group):
        k_pages = rest[s * n_pages:(s + 1) * n_pages]
        v_pages = rest[(group + s) * n_pages:(group + s + 1) * n_pages]
        o_ref[s] = _attn_one_sequence(q_ref[s].astype(F32), kn_ref[s], vn_ref[s], k_pages, v_pages,
                                      g_ref[...], lam, li)


def _attn_one_sequence(q, kn, vn, k_pages, v_pages, g, lam, li):
    n_pages = len(k_pages)
    pr = k_pages[0].shape[1]
    row = lax.broadcasted_iota(jnp.int32, (SCORE_ROWS, HEAD_W), 0)
    lane = lax.broadcasted_iota(jnp.int32, (SCORE_ROWS, HEAD_W), 1)
    qrows = jnp.zeros((SCORE_ROWS, HEAD_W), F32)
    knrows = jnp.zeros((SCORE_ROWS, HEAD_W), F32)
    vnrows = jnp.zeros((SCORE_ROWS, HEAD_W), F32)
    for h in range(ATT_HEADS):
        cs = slice(h * HEAD_W, (h + 1) * HEAD_W)
        in_head = row // 2 == h
        qrows = jnp.where(in_head & (lane // ATT_HD == row % 2), q[:, cs], qrows)
        knrows = jnp.where(in_head, kn[:, cs], knrows)
        vnrows = jnp.where(in_head, vn[:, cs], vnrows)
    qb = qrows.astype(BF16)
    s_past = jnp.concatenate([_nt_dot(qb, kp[0].astype(BF16)) for kp in k_pages], axis=1)
    srow = lax.broadcasted_iota(jnp.int32, s_past.shape, 0)
    scol = lax.broadcasted_iota(jnp.int32, s_past.shape, 1)
    own = (scol % ATT_HEADS) == (srow // 2)
    s_past = jnp.where(own, s_past, -jnp.inf)
    s_new = jnp.sum(qrows * knrows, axis=-1, keepdims=True)
    m = jnp.maximum(jnp.max(s_past, axis=-1, keepdims=True), s_new)
    p = jnp.exp2(s_past - m)
    p_new = jnp.exp2(s_new - m)
    l = jnp.sum(p, axis=-1, keepdims=True) + p_new
    acc = p_new * vnrows
    for j in range(n_pages):
        acc = acc + jnp.dot(p[:, j * pr:(j + 1) * pr].astype(BF16), v_pages[j][0].astype(BF16),
                            preferred_element_type=F32)
    o = acc / l
    outs = []
    for h in range(ATT_HEADS):
        d = o[2 * h:2 * h + 1, :] - lam * o[2 * h + 1:2 * h + 2, :]
        outs.append(_rms(d, g) * (1.0 - _lambda_init(li)))
    return jnp.concatenate(outs, axis=1).astype(BF16)


def _attn_sample(page_table, lams, aq, ak, av, g_subln, cache_k, cache_v, li, group):
    n_seq, n_pages = page_table.shape
    depth, n_pool, page = cache_k.shape[:3]
    pr = page * ATT_HEADS
    ck = cache_k.reshape(depth * n_pool, pr, HEAD_W)
    cv = cache_v.reshape(depth * n_pool, pr, HEAD_W)
    base = li * n_pool

    def page_spec(s, j):
        return pl.BlockSpec((1, pr, HEAD_W), lambda i, pt: (base + pt[i * group + s, j], 0, 0))

    pages = [page_spec(s, j) for s in range(group) for j in range(n_pages)]
    tok = pl.BlockSpec((group, 1, ATT_WIDTH), lambda i, pt: (i, 0, 0))
    grid_spec = pltpu.PrefetchScalarGridSpec(
        num_scalar_prefetch=1,
        grid=(n_seq // group,),
        in_specs=[pl.BlockSpec(lams.shape, lambda i, pt: (0, 0)), tok, tok, tok,
                  pl.BlockSpec((1, HEAD_W), lambda i, pt: (0, 0))] + pages * 2,
        out_specs=tok,
    )
    out = pl.pallas_call(
        functools.partial(_attn_sample_kernel, li=li, n_pages=n_pages, group=group),
        grid_spec=grid_spec,
        out_shape=jax.ShapeDtypeStruct((n_seq, 1, ATT_WIDTH), BF16),
        compiler_params=pltpu.CompilerParams(dimension_semantics=("arbitrary",),
                                             vmem_limit_bytes=VMEM_LIMIT),
        name="attn_sample",
    )(page_table, lams, aq.reshape(n_seq, 1, ATT_WIDTH), ak.reshape(n_seq, 1, ATT_WIDTH),
      av.reshape(n_seq, 1, ATT_WIDTH), g_subln,
      *([ck] * (n_pages * group)), *([cv] * (n_pages * group)))
    return out.reshape(n_seq, ATT_WIDTH)


def _split3(x):
    hi = x.astype(BF16)
    r = x - hi.astype(F32)
    mid = r.astype(BF16)
    lo = (r - mid.astype(F32)).astype(BF16)
    return hi, mid, lo


def _hgrn_prompt_kernel(tri_ref, q_ref, k_ref, v_ref, lf_ref, gate_ref, grec_ref,
                        o_ref, s_ref, st_ref, b_ref, acc_ref, *, tt):
    t = pl.program_id(1)
    nt = pl.num_programs(1)
    c = HGRN_CHUNK

    @pl.when(t == 0)
    def _():
        st_ref[...] = jnp.zeros_like(st_ref)

    tri = tri_ref[...]
    for r0 in range(0, tt, HGRN_CUM_BLOCK):
        lf2 = lf_ref[r0:r0 + HGRN_CUM_BLOCK, :] * LOG2E
        hi, mid, lo = _split3(lf2)
        b_ref[r0:r0 + HGRN_CUM_BLOCK, :] = (
            jnp.dot(tri, hi, preferred_element_type=F32)
            + jnp.dot(tri, mid, preferred_element_type=F32)
            + jnp.dot(tri, lo, preferred_element_type=F32))

    rowi = lax.broadcasted_iota(jnp.int32, (c, c), 0)
    coli = lax.broadcasted_iota(jnp.int32, (c, c), 1)
    pair = jnp.where(coli > rowi, -1, BAND + len(HGRN_LEVELS) - 1)
    for level in reversed(range(len(HGRN_LEVELS))):
        blk = HGRN_LEVELS[level]
        cls = rowi - coli if level == 0 else BAND + level - 1
        pair = jnp.where((rowi // blk == coli // blk) & (coli <= rowi), cls, pair)

    heads = [slice(h * REC_DK, (h + 1) * REC_DK) for h in range(REC_HEADS)]

    def chunk_matmuls(r0):
        qs = [q_ref[pl.ds(r0, c), cs].astype(F32) for cs in heads]
        ks = [k_ref[pl.ds(r0, c), cs].astype(F32) for cs in heads]
        bs = [b_ref[pl.ds(r0, c), cs] for cs in heads]
        o_state, cross = [], []
        for h, cs in enumerate(heads):
            q, k, b = qs[h], ks[h], bs[h]
            vb = v_ref[pl.ds(r0, c), cs]
            st = st_ref[h]
            o_state.append(_nt_dot((q * jnp.exp2(b)).astype(BF16), st.astype(BF16)))
            bl = b[c - 1:c, :]
            kd = (k * jnp.exp2(bl - b)).astype(BF16)
            st_ref[h] = st * jnp.exp2(bl) + _tn_dot(vb, kd)
            per_level = []
            for half in HGRN_LEVELS:
                blk = 2 * half
                if half % SUBLANES:
                    ref = jnp.concatenate(
                        [jnp.broadcast_to(b[g0 * blk + half - 1:g0 * blk + half, :], (blk, REC_DK))
                         for g0 in range(c // blk)], axis=0)
                    d = b - ref
                    qe = jnp.minimum(d, 0.0)
                    qf = q * jnp.exp2(qe)
                    kf = k * jnp.exp2(qe - d)
                else:
                    q_rows, k_rows = [], []
                    for g0 in range(c // blk):
                        lo, mid, hi = g0 * blk, g0 * blk + half, (g0 + 1) * blk
                        ref = b[mid - 1:mid, :]
                        k_rows += [k[lo:mid] * jnp.exp2(ref - b[lo:mid]), k[mid:hi]]
                        q_rows += [q[lo:mid], q[mid:hi] * jnp.exp2(b[mid:hi] - ref)]
                    qf = jnp.concatenate(q_rows, axis=0)
                    kf = jnp.concatenate(k_rows, axis=0)
                per_level.append(_nt_dot(qf.astype(BF16), kf.astype(BF16)))
            cross.append(per_level)
        return qs, ks, o_state, cross

    def chunk_finish(r0, qs, ks, o_state, cross):
        for h, cs in enumerate(heads):
            q = qs[h]
            f = jnp.exp(lf_ref[pl.ds(r0, c), cs])
            g = ks[h]
            a = jnp.zeros((c, c), F32)
            for dist in range(BAND):
                if dist > 0:
                    g3 = g.reshape(c // SUBLANES, SUBLANES, REC_DK)
                    g = f * pltpu.roll(g3, 1, 1).reshape(c, REC_DK)
                w = jnp.sum(q * g, axis=-1, keepdims=True)
                a = jnp.where(pair == dist, w, a)
            for level in range(len(HGRN_LEVELS)):
                a = jnp.where(pair == BAND + level, cross[h][level], a)
            vb = v_ref[pl.ds(r0, c), cs]
            acc_ref[pl.ds(r0, c), cs] = o_state[h] + jnp.dot(a.astype(BF16), vb,
                                                             preferred_element_type=F32)

    pending = None
    for r0 in range(0, tt, c):
        cur = chunk_matmuls(r0)
        if pending is not None:
            chunk_finish(*pending)
        pending = (r0,) + cur
    chunk_finish(*pending)

    o_all = acc_ref[...]
    o_ref[...] = (_rms(o_all, grec_ref[...]) * gate_ref[...].astype(F32)).astype(BF16)

    @pl.when(t == nt - 1)
    def _():
        for h in range(REC_HEADS):
            s_ref[0, h] = st_ref[h].T


def _hgrn_tri():
    i = np.arange(HGRN_CUM_BLOCK)
    m = (i[:, None] >= i[None, :]) & ((i[:, None] // HGRN_CHUNK) == (i[None, :] // HGRN_CHUNK))
    return jnp.asarray(m.astype(np.float32), dtype=BF16)


def _hgrn_prompt(rq, rk, rv, lf, rg, g_rec, batch, seq, tt):
    nt = seq // tt
    n = batch * seq
    tile = pl.BlockSpec((tt, REC_WIDTH), lambda b, t: (b * nt + t, 0))
    return pl.pallas_call(
        functools.partial(_hgrn_prompt_kernel, tt=tt),
        grid=(batch, nt),
        in_specs=[pl.BlockSpec((HGRN_CUM_BLOCK, HGRN_CUM_BLOCK), lambda b, t: (0, 0)),
                  tile, tile, tile, tile, tile,
                  pl.BlockSpec((1, REC_WIDTH), lambda b, t: (0, 0))],
        out_specs=[tile, pl.BlockSpec((1, REC_HEADS, REC_DK, REC_DK), lambda b, t: (b, 0, 0, 0))],
        out_shape=[jax.ShapeDtypeStruct((n, REC_WIDTH), BF16),
                   jax.ShapeDtypeStruct((batch, REC_HEADS, REC_DK, REC_DK), F32)],
        scratch_shapes=[pltpu.VMEM((REC_HEADS, REC_DK, REC_DK), F32),
                        pltpu.VMEM((tt, REC_WIDTH), F32),
                        pltpu.VMEM((tt, REC_WIDTH), F32)],
        compiler_params=pltpu.CompilerParams(dimension_semantics=("arbitrary", "arbitrary"),
                                             vmem_limit_bytes=VMEM_LIMIT),
        name="hgrn_prompt",
    )(_hgrn_tri(), rq, rk, rv, lf, rg, g_rec)


def _hgrn_sample_kernel(q_ref, k_ref, v_ref, lf_ref, gate_ref, grec_ref, s0_ref, o_ref, s_ref, *, group):
    rowi = lax.broadcasted_iota(jnp.int32, (group, REC_DK), 0)
    outs = []
    for h in range(REC_HEADS):
        cs = slice(h * REC_DK, (h + 1) * REC_DK)
        q = q_ref[:, cs]
        k = k_ref[:, cs]
        v = v_ref[:, cs]
        ft = jnp.exp(lf_ref[:, cs]).T
        outer = []
        for n in range(group):
            kn = jnp.where(rowi == n, k, jnp.zeros_like(k))
            outer.append(_tn_dot(kn, v))
        new = []
        for n in range(group):
            st = s0_ref[n, h] * ft[:, n:n + 1] + outer[n]
            s_ref[n, h] = st
            new.append(st.astype(BF16))
        o_h = jnp.zeros((group, REC_DK), F32)
        for n in range(group):
            o_h = jnp.where(rowi == n, jnp.dot(q, new[n], preferred_element_type=F32), o_h)
        outs.append(o_h)
    o = jnp.concatenate(outs, axis=1)
    o_ref[...] = (_rms(o, grec_ref[...]) * gate_ref[...].astype(F32)).astype(BF16)


def _hgrn_sample(rq, rk, rv, lf, rg, g_rec, s0, group):
    n = rq.shape[0]
    tile = pl.BlockSpec((group, REC_WIDTH), lambda i: (i, 0))
    st = pl.BlockSpec((group, REC_HEADS, REC_DK, REC_DK), lambda i: (i, 0, 0, 0))
    return pl.pallas_call(
        functools.partial(_hgrn_sample_kernel, group=group),
        grid=(n // group,),
        in_specs=[tile, tile, tile, tile, tile, pl.BlockSpec((1, REC_WIDTH), lambda i: (0, 0)), st],
        out_specs=[tile, st],
        out_shape=[jax.ShapeDtypeStruct((n, REC_WIDTH), BF16),
                   jax.ShapeDtypeStruct(s0.shape, F32)],
        compiler_params=pltpu.CompilerParams(dimension_semantics=("arbitrary",),
                                             vmem_limit_bytes=VMEM_LIMIT),
        name="hgrn_sample",
    )(rq, rk, rv, lf, rg, g_rec, s0)


FF_BLOCK = 1024
TAIL_VMEM_LIMIT = V7X_VMEM_BYTES * 7 // 8


def _tail_kernel(x_ref, att_ref, rec_ref, p_ref, wo_ref, gffn_ref, wup_ref, wdn_ref,
                 wg_ref, wp_ref, gfin_ref, y_ref, *, final):
    mix = (jnp.dot(att_ref[...], wo_ref[:ATT_WIDTH, :], preferred_element_type=F32)
           + jnp.dot(rec_ref[...], wo_ref[ATT_WIDTH:, :], preferred_element_type=F32))
    x = x_ref[...] + mix
    h = _rms(x, gffn_ref[...]).astype(BF16)
    for c0 in range(0, D_FF, FF_BLOCK):
        up = jnp.dot(h, wup_ref[:, c0:c0 + FF_BLOCK], preferred_element_type=F32)
        act = jnp.square(jnp.maximum(up, 0.0)).astype(BF16)
        x = x + jnp.dot(act, wdn_ref[c0:c0 + FF_BLOCK, :], preferred_element_type=F32)
    gate = jax.nn.sigmoid(jnp.dot(x.astype(BF16), wg_ref[...], preferred_element_type=F32))
    emb = jnp.dot(p_ref[...].astype(BF16), wp_ref[...], preferred_element_type=F32)
    x = x + gate * emb
    y_ref[...] = _rms(x, gfin_ref[...]) if final else x


def _tail(x, att, rec, p, w_out, g_ffn, w_up, w_down, w_gate, w_proj, g_final, final, tm):
    n = x.shape[0]
    row = lambda i: (i, 0)
    const = lambda i: (0, 0)

    def resident(a):
        return pl.BlockSpec(a.shape, const, pipeline_mode=pl.Buffered(1))

    return pl.pallas_call(
        functools.partial(_tail_kernel, final=final),
        grid=(n // tm,),
        in_specs=[pl.BlockSpec((tm, D_MODEL), row),
                  pl.BlockSpec((tm, ATT_WIDTH), row),
                  pl.BlockSpec((tm, REC_WIDTH), row),
                  pl.BlockSpec((tm, PLE_DIM), row),
                  resident(w_out), resident(g_ffn), resident(w_up), resident(w_down),
                  resident(w_gate), resident(w_proj), resident(g_final)],
        out_specs=pl.BlockSpec((tm, D_MODEL), row),
        out_shape=jax.ShapeDtypeStruct((n, D_MODEL), F32),
        compiler_params=pltpu.CompilerParams(dimension_semantics=("arbitrary",),
                                             vmem_limit_bytes=TAIL_VMEM_LIMIT),
        name="tail",
    )(x, att, rec, p, w_out, g_ffn, w_up, w_down, w_gate, w_proj, g_final)


def kernel(x_prompt, x_sample, cache_k, cache_v, state_hgrn, page_table, p_prompt, p_sample,
           w_in, lambda_q1, lambda_k1, lambda_q2, lambda_k2, g_subln, hgrn_lb, g_rec, w_out,
           g_mix, g_ffn, w_up, w_down, w_ple_gate, w_ple_proj, g_final):
    batch, seq, d = x_prompt.shape
    n_seq = x_sample.shape[0]
    depth = w_in.shape[0]
    assert d == D_MODEL and w_in.shape[2] == N_PROJ * ATT_WIDTH and x_sample.shape[1] == 1
    assert cache_k.shape[3:] == (ATT_HEADS, HEAD_W) and state_hgrn.shape[2:] == (REC_HEADS, REC_DK, REC_DK)
    n_p = batch * seq

    hp = x_prompt.reshape(n_p, d)
    hs = x_sample.reshape(n_seq, d)
    row2 = lambda a: a.reshape(1, -1)
    outs = [[] for _ in range(6)]
    for li in range(depth):
        last = li == depth - 1
        w_in_b = w_in[li].astype(BF16)
        w_out_b = w_out[li].astype(BF16)
        w_up_b = w_up[li].astype(BF16)
        w_dn_b = w_down[li].astype(BF16)
        w_g_b = w_ple_gate[li].astype(BF16)
        w_p_b = w_ple_proj[li].astype(BF16)
        lams = jnp.stack([lambda_q1[li], lambda_k1[li], lambda_q2[li], lambda_k2[li]], axis=0)
        g_sub = row2(g_subln[li])
        g_r = row2(g_rec[li])
        tail_w = (w_out_b, row2(g_ffn[li]), w_up_b, w_dn_b, w_g_b, w_p_b, row2(g_final))

        aq, ak, av, akb, avb, rq, lf, rk, rv, rg = _inproj(hp, row2(g_mix[li]), w_in_b, hgrn_lb, li, tm=1024)
        att = _attn_prompt(lams, aq, akb, avb, g_sub, li, batch, seq, tq=256, heads_per_step=2)
        rec, s_p = _hgrn_prompt(rq, rk, rv, lf, rg, g_r, batch, seq, tt=512)
        hp = _tail(hp, att, rec, p_prompt[li].reshape(n_p, PLE_DIM), *tail_w, final=last, tm=1024)
        outs[0].append(ak.reshape(batch, seq, ATT_HEADS, HEAD_W))
        outs[1].append(av.reshape(batch, seq, ATT_HEADS, HEAD_W))
        outs[2].append(s_p)

        aq, ak, av, _, _, rq, lf, rk, rv, rg = _inproj(hs, row2(g_mix[li]), w_in_b, hgrn_lb, li, tm=n_seq)
        att = _attn_sample(page_table, lams, aq, ak, av, g_sub, cache_k, cache_v, li, group=2)
        rec, s_s = _hgrn_sample(rq, rk, rv, lf, rg, g_r, state_hgrn[li], group=16)
        hs = _tail(hs, att, rec, p_sample[li].reshape(n_seq, PLE_DIM), *tail_w, final=last, tm=n_seq)
        outs[3].append(ak.reshape(n_seq, 1, ATT_HEADS, HEAD_W))
        outs[4].append(av.reshape(n_seq, 1, ATT_HEADS, HEAD_W))
        outs[5].append(s_s)

    k_p, v_p, s_p, k_s, v_s, s_s = [jnp.stack(o, axis=0) for o in outs]
    return (hp.reshape(batch, seq, d), hs.reshape(n_seq, 1, d), k_p, v_p, s_p, k_s, v_s, s_s)
```

```python
import functools
import math

import jax
import jax.numpy as jnp
import numpy as np
from jax import lax
from jax.experimental import pallas as pl
from jax.experimental.pallas import tpu as pltpu

F32 = jnp.float32
BF16 = jnp.bfloat16

D_MODEL = 1024
ATT_HEADS = 4
ATT_HD = 64
HEAD_W = 2 * ATT_HD
ATT_WIDTH = ATT_HEADS * HEAD_W
REC_HEADS = 4
REC_DK = 128
REC_WIDTH = REC_HEADS * REC_DK
N_PROJ = 7
PLE_DIM = 256
D_FF = 4 * D_MODEL
EPS = 1e-6
LOG2E = math.log2(math.e)
ATT_SCALE = ATT_HD ** -0.5 * LOG2E

V7X_VMEM_BYTES = 64 * 1024 * 1024
VMEM_LIMIT = V7X_VMEM_BYTES * 13 // 16

HGRN_CHUNK = 64
HGRN_CUM_BLOCK = 256
BAND = 4
HGRN_LEVELS = (4, 8, 16, 32)
SUBLANES = 8


def _lambda_init(li):
    return 0.8 - 0.6 * math.exp(-0.3 * li)


def _rms(x, g):
    return x * lax.rsqrt(jnp.mean(x * x, axis=-1, keepdims=True) + EPS) * g


def _silu(x):
    return x * jax.nn.sigmoid(x)


def _nt_dot(a, b):
    return lax.dot_general(a, b, (((1,), (1,)), ((), ())), preferred_element_type=F32)


def _tn_dot(a, b):
    return lax.dot_general(a, b, (((0,), (0,)), ((), ())), preferred_element_type=F32)


def _lam_from(lams, li):
    a = jnp.sum(lams[0:1, :] * lams[1:2, :], axis=-1, keepdims=True)
    b = jnp.sum(lams[2:3, :] * lams[3:4, :], axis=-1, keepdims=True)
    return jnp.exp(a) - jnp.exp(b) + _lambda_init(li)


def _inproj_kernel(x_ref, g_ref, w_ref, lb_ref,
                   aq_ref, ak_ref, av_ref, akb_ref, avb_ref,
                   rq_ref, lf_ref, rk_ref, rv_ref, rg_ref, *, li):
    h = _rms(x_ref[...], g_ref[...]).astype(BF16)

    def proj(i):
        return jnp.dot(h, w_ref[:, i * ATT_WIDTH:(i + 1) * ATT_WIDTH], preferred_element_type=F32)

    lbp = lb_ref[...]
    e = jnp.exp(lbp - jnp.max(lbp, axis=0, keepdims=True))
    lb = jnp.sum(e[:li + 1], axis=0, keepdims=True) / jnp.sum(e, axis=0, keepdims=True)
    f = lb + (1.0 - lb) * jax.nn.sigmoid(proj(4))
    lf_ref[...] = jnp.log(f)
    rk_ref[...] = (1.0 - f).astype(BF16)
    rq_ref[...] = _silu(proj(3)).astype(BF16)
    rg_ref[...] = _silu(proj(6)).astype(BF16)
    tm = x_ref.shape[0]
    for full_ref, half_ref, val in ((ak_ref, akb_ref, proj(1)), (av_ref, avb_ref, proj(2))):
        half_ref[...] = val.astype(BF16)
        for hd in range(ATT_HEADS):
            full_ref[pl.ds(hd, tm, stride=ATT_HEADS), :] = val[:, hd * HEAD_W:(hd + 1) * HEAD_W]
    aq_ref[...] = (proj(0) * ATT_SCALE).astype(BF16)
    rv_ref[...] = proj(5).astype(BF16)


def _inproj(x, g_mix, w_in, hgrn_lb, li, tm):
    n = x.shape[0]
    grid = (n // tm,)
    row = lambda i: (i, 0)
    const = lambda i: (0, 0)
    wide = pl.BlockSpec((tm, ATT_WIDTH), row)
    heads = pl.BlockSpec((tm * ATT_HEADS, HEAD_W), row)
    flat = lambda dt: jax.ShapeDtypeStruct((n, ATT_WIDTH), dt)
    by_head = jax.ShapeDtypeStruct((n * ATT_HEADS, HEAD_W), F32)
    return pl.pallas_call(
        functools.partial(_inproj_kernel, li=li),
        grid=grid,
        in_specs=[pl.BlockSpec((tm, D_MODEL), row),
                  pl.BlockSpec((1, D_MODEL), const),
                  pl.BlockSpec(w_in.shape, const, pipeline_mode=pl.Buffered(1)),
                  pl.BlockSpec(hgrn_lb.shape, const)],
        out_specs=[wide, heads, heads] + [wide] * 7,
        out_shape=[flat(BF16), by_head, by_head, flat(BF16), flat(BF16),
                   flat(BF16), flat(F32), flat(BF16), flat(BF16), flat(BF16)],
        compiler_params=pltpu.CompilerParams(dimension_semantics=("arbitrary",),
                                             vmem_limit_bytes=VMEM_LIMIT),
        name="inproj",
    )(x, g_mix, w_in, hgrn_lb)


def _attn_prompt_kernel(lams_ref, q_ref, k_ref, v_ref, g_ref, o_ref, s_ref, *, li, tq):
    seq = q_ref.shape[0]
    lane_blocks = tq // HEAD_W
    lam = _lam_from(lams_ref[...], li)
    lane = lax.broadcasted_iota(jnp.int32, (tq, HEAD_W), 1)
    r = lax.broadcasted_iota(jnp.int32, (2 * tq, tq), 0)
    c = lax.broadcasted_iota(jnp.int32, (2 * tq, tq), 1)
    visible = c <= jnp.where(r >= tq, r - tq, r)

    def fold(x, op):
        out = x[:, :HEAD_W]
        for b in range(1, lane_blocks):
            out = op(out, x[:, b * HEAD_W:(b + 1) * HEAD_W])
        return out

    ring = s_ref.shape[0]

    def pass1(hh, qi, base):
        cs = slice(hh * HEAD_W, (hh + 1) * HEAD_W)
        q = q_ref[qi * tq:(qi + 1) * tq, cs]
        zero = jnp.zeros_like(q)
        qs = jnp.concatenate([jnp.where(lane < ATT_HD, q, zero),
                              jnp.where(lane >= ATT_HD, q, zero)], axis=0)
        m = None
        for j in range(qi + 1):
            s = _nt_dot(qs, k_ref[j * tq:(j + 1) * tq, cs])
            if j == qi:
                s = jnp.where(visible, s, -jnp.inf)
            s_ref[(base + j) % ring] = s
            fm = fold(s, jnp.maximum)
            m = fm if m is None else jnp.maximum(m, fm)
        return jnp.broadcast_to(jnp.max(m, axis=-1, keepdims=True), m.shape)

    def pass2(hh, qi, base, mb):
        cs = slice(hh * HEAD_W, (hh + 1) * HEAD_W)
        l = jnp.zeros((2 * tq, HEAD_W), F32)
        acc = jnp.zeros((2 * tq, HEAD_W), F32)
        for j in range(qi + 1):
            s = s_ref[(base + j) % ring]
            p = jnp.concatenate([jnp.exp2(s[:, b * HEAD_W:(b + 1) * HEAD_W] - mb)
                                 for b in range(lane_blocks)], axis=1)
            l = l + fold(p, jnp.add)
            acc = acc + jnp.dot(p.astype(BF16), v_ref[j * tq:(j + 1) * tq, cs],
                                preferred_element_type=F32)
        o = acc / jnp.sum(l, axis=-1, keepdims=True)
        d = o[:tq] - lam * o[tq:]
        o_ref[qi * tq:(qi + 1) * tq, cs] = (_rms(d, g_ref[...]) * (1.0 - _lambda_init(li))).astype(BF16)

    pending = None
    base = 0
    for qi in range(seq // tq):
        for hh in range(q_ref.shape[1] // HEAD_W):
            mb = pass1(hh, qi, base)
            if pending is not None:
                pass2(*pending)
            pending = (hh, qi, base, mb)
            base += qi + 1
    pass2(*pending)


def _attn_prompt(lams, aq, akb, avb, g_subln, li, batch, seq, tq, heads_per_step):
    nq = seq // tq
    n = batch * seq
    group = pl.BlockSpec((seq, heads_per_step * HEAD_W), lambda b, h: (b, h))
    return pl.pallas_call(
        functools.partial(_attn_prompt_kernel, li=li, tq=tq),
        grid=(batch, ATT_HEADS // heads_per_step),
        in_specs=[pl.BlockSpec(lams.shape, lambda b, h: (0, 0)),
                  group, group, group,
                  pl.BlockSpec((1, HEAD_W), lambda b, h: (0, 0))],
        out_specs=group,
        out_shape=jax.ShapeDtypeStruct((n, ATT_WIDTH), BF16),
        scratch_shapes=[pltpu.VMEM((2 * nq, 2 * tq, tq), F32)],
        compiler_params=pltpu.CompilerParams(
            dimension_semantics=("arbitrary", "arbitrary"),
            vmem_limit_bytes=VMEM_LIMIT),
        name="attn_prompt",
    )(lams, aq, akb, avb, g_subln)


SCORE_ROWS = 16


def _attn_sample_kernel(pt_ref, lams_ref, q_ref, kn_ref, vn_ref, g_ref, *rest, li, n_pages, group):
    del pt_ref
    o_ref = rest[2 * n_pages * group]
    lam = _lam_from(lams_ref[...], li)
    for s in range(group):
        k_pages = rest[s * n_pages:(s + 1) * n_pages]
        v_pages = rest[(group + s) * n_pages:(group + s + 1) * n_pages]
        o_ref[s] = _attn_one_sequence(q_ref[s].astype(F32), kn_ref[s], vn_ref[s], k_pages, v_pages,
                                      g_ref[...], lam, li)


def _attn_one_sequence(q, kn, vn, k_pages, v_pages, g, lam, li):
    n_pages = len(k_pages)
    pr = k_pages[0].shape[1]
    row = lax.broadcasted_iota(jnp.int32, (SCORE_ROWS, HEAD_W), 0)
    lane = lax.broadcasted_iota(jnp.int32, (SCORE_ROWS, HEAD_W), 1)
    qrows = jnp.zeros((SCORE_ROWS, HEAD_W), F32)
    knrows = jnp.zeros((SCORE_ROWS, HEAD_W), F32)
    vnrows = jnp.zeros((SCORE_ROWS, HEAD_W), F32)
    for h in range(ATT_HEADS):
        cs = slice(h * HEAD_W, (h + 1) * HEAD_W)
        in_head = row // 2 == h
        qrows = jnp.where(in_head & (lane // ATT_HD == row % 2), q[:, cs], qrows)
        knrows = jnp.where(in_head, kn[:, cs], knrows)
        vnrows = jnp.where(in_head, vn[:, cs], vnrows)
    qb = qrows.astype(BF16)
    s_past = jnp.concatenate([_nt_dot(qb, kp[0].astype(BF16)) for kp in k_pages], axis=1)
    srow = lax.broadcasted_iota(jnp.int32, s_past.shape, 0)
    scol = lax.broadcasted_iota(jnp.int32, s_past.shape, 1)
    own = (scol % ATT_HEADS) == (srow // 2)
    s_past = jnp.where(own, s_past, -jnp.inf)
    s_new = jnp.sum(qrows * knrows, axis=-1, keepdims=True)
    m = jnp.maximum(jnp.max(s_past, axis=-1, keepdims=True), s_new)
    p = jnp.exp2(s_past - m)
    p_new = jnp.exp2(s_new - m)
    l = jnp.sum(p, axis=-1, keepdims=True) + p_new
    acc = p_new * vnrows
    for j in range(n_pages):
        acc = acc + jnp.dot(p[:, j * pr:(j + 1) * pr].astype(BF16), v_pages[j][0].astype(BF16),
                            preferred_element_type=F32)
    o = acc / l
    outs = []
    for h in range(ATT_HEADS):
        d = o[2 * h:2 * h + 1, :] - lam * o[2 * h + 1:2 * h + 2, :]
        outs.append(_rms(d, g) * (1.0 - _lambda_init(li)))
    return jnp.concatenate(outs, axis=1).astype(BF16)


def _attn_sample(page_table, lams, aq, ak, av, g_subln, cache_k, cache_v, li, group):
    n_seq, n_pages = page_table.shape
    depth, n_pool, page = cache_k.shape[:3]
    pr = page * ATT_HEADS
    ck = cache_k.reshape(depth * n_pool, pr, HEAD_W)
    cv = cache_v.reshape(depth * n_pool, pr, HEAD_W)
    base = li * n_pool

    def page_spec(s, j):
        return pl.BlockSpec((1, pr, HEAD_W), lambda i, pt: (base + pt[i * group + s, j], 0, 0))

    pages = [page_spec(s, j) for s in range(group) for j in range(n_pages)]
    tok = pl.BlockSpec((group, 1, ATT_WIDTH), lambda i, pt: (i, 0, 0))
    grid_spec = pltpu.PrefetchScalarGridSpec(
        num_scalar_prefetch=1,
        grid=(n_seq // group,),
        in_specs=[pl.BlockSpec(lams.shape, lambda i, pt: (0, 0)), tok, tok, tok,
                  pl.BlockSpec((1, HEAD_W), lambda i, pt: (0, 0))] + pages * 2,
        out_specs=tok,
    )
    out = pl.pallas_call(
        functools.partial(_attn_sample_kernel, li=li, n_pages=n_pages, group=group),
        grid_spec=grid_spec,
        out_shape=jax.ShapeDtypeStruct((n_seq, 1, ATT_WIDTH), BF16),
        compiler_params=pltpu.CompilerParams(dimension_semantics=("arbitrary",),
                                             vmem_limit_bytes=VMEM_LIMIT),
        name="attn_sample",
    )(page_table, lams, aq.reshape(n_seq, 1, ATT_WIDTH), ak.reshape(n_seq, 1, ATT_WIDTH),
      av.reshape(n_seq, 1, ATT_WIDTH), g_subln,
      *([ck] * (n_pages * group)), *([cv] * (n_pages * group)))
    return out.reshape(n_seq, ATT_WIDTH)


def _split3(x):
    hi = x.astype(BF16)
    r = x - hi.astype(F32)
    mid = r.astype(BF16)
    lo = (r - mid.astype(F32)).astype(BF16)
    return hi, mid, lo


def _hgrn_prompt_kernel(tri_ref, q_ref, k_ref, v_ref, lf_ref, gate_ref, grec_ref,
                        o_ref, s_ref, st_ref, b_ref, acc_ref, *, tt):
    t = pl.program_id(1)
    nt = pl.num_programs(1)
    c = HGRN_CHUNK

    @pl.when(t == 0)
    def _():
        st_ref[...] = jnp.zeros_like(st_ref)

    tri = tri_ref[...]
    for r0 in range(0, tt, HGRN_CUM_BLOCK):
        lf2 = lf_ref[r0:r0 + HGRN_CUM_BLOCK, :] * LOG2E
        hi, mid, lo = _split3(lf2)
        b_ref[r0:r0 + HGRN_CUM_BLOCK, :] = (
            jnp.dot(tri, hi, preferred_element_type=F32)
            + jnp.dot(tri, mid, preferred_element_type=F32)
            + jnp.dot(tri, lo, preferred_element_type=F32))

    rowi = lax.broadcasted_iota(jnp.int32, (c, c), 0)
    coli = lax.broadcasted_iota(jnp.int32, (c, c), 1)
    pair = jnp.where(coli > rowi, -1, BAND + len(HGRN_LEVELS) - 1)
    for level in reversed(range(len(HGRN_LEVELS))):
        blk = HGRN_LEVELS[level]
        cls = rowi - coli if level == 0 else BAND + level - 1
        pair = jnp.where((rowi // blk == coli // blk) & (coli <= rowi), cls, pair)

    heads = [slice(h * REC_DK, (h + 1) * REC_DK) for h in range(REC_HEADS)]

    def chunk_matmuls(r0):
        qs = [q_ref[pl.ds(r0, c), cs].astype(F32) for cs in heads]
        ks = [k_ref[pl.ds(r0, c), cs].astype(F32) for cs in heads]
        bs = [b_ref[pl.ds(r0, c), cs] for cs in heads]
        o_state, cross = [], []
        for h, cs in enumerate(heads):
            q, k, b = qs[h], ks[h], bs[h]
            vb = v_ref[pl.ds(r0, c), cs]
            st = st_ref[h]
            o_state.append(_nt_dot((q * jnp.exp2(b)).astype(BF16), st.astype(BF16)))
            bl = b[c - 1:c, :]
            kd = (k * jnp.exp2(bl - b)).astype(BF16)
            st_ref[h] = st * jnp.exp2(bl) + _tn_dot(vb, kd)
            per_level = []
            for half in HGRN_LEVELS:
                blk = 2 * half
                if half % SUBLANES:
                    ref = jnp.concatenate(
                        [jnp.broadcast_to(b[g0 * blk + half - 1:g0 * blk + half, :], (blk, REC_DK))
                         for g0 in range(c // blk)], axis=0)
                    d = b - ref
                    qe = jnp.minimum(d, 0.0)
                    qf = q * jnp.exp2(qe)
                    kf = k * jnp.exp2(qe - d)
                else:
                    q_rows, k_rows = [], []
                    for g0 in range(c // blk):
                        lo, mid, hi = g0 * blk, g0 * blk + half, (g0 + 1) * blk
                        ref = b[mid - 1:mid, :]
                        k_rows += [k[lo:mid] * jnp.exp2(ref - b[lo:mid]), k[mid:hi]]
                        q_rows += [q[lo:mid], q[mid:hi] * jnp.exp2(b[mid:hi] - ref)]
                    qf = jnp.concatenate(q_rows, axis=0)
                    kf = jnp.concatenate(k_rows, axis=0)
                per_level.append(_nt_dot(qf.astype(BF16), kf.astype(BF16)))
            cross.append(per_level)
        return qs, ks, o_state, cross

    def chunk_finish(r0, qs, ks, o_state, cross):
        for h, cs in enumerate(heads):
            q = qs[h]
            f = jnp.exp(lf_ref[pl.ds(r0, c), cs])
            g = ks[h]
            a = jnp.zeros((c, c), F32)
            for dist in range(BAND):
                if dist > 0:
                    g3 = g.reshape(c // SUBLANES, SUBLANES, REC_DK)
                    g = f * pltpu.roll(g3, 1, 1).reshape(c, REC_DK)
                w = jnp.sum(q * g, axis=-1, keepdims=True)
                a = jnp.where(pair == dist, w, a)
            for level in range(len(HGRN_LEVELS)):
                a = jnp.where(pair == BAND + level, cross[h][level], a)
            vb = v_ref[pl.ds(r0, c), cs]
            acc_ref[pl.ds(r0, c), cs] = o_state[h] + jnp.dot(a.astype(BF16), vb,
                                                             preferred_element_type=F32)

    pending = None
    for r0 in range(0, tt, c):
        cur = chunk_matmuls(r0)
        if pending is not None:
            chunk_finish(*pending)
        pending = (r0,) + cur
    chunk_finish(*pending)

    o_all = acc_ref[...]
    o_ref[...] = (_rms(o_all, grec_ref[...]) * gate_ref[...].astype(F32)).astype(BF16)

    @pl.when(t == nt - 1)
    def _():
        for h in range(REC_HEADS):
            s_ref[0, h] = st_ref[h].T


def _hgrn_tri():
    i = np.arange(HGRN_CUM_BLOCK)
    m = (i[:, None] >= i[None, :]) & ((i[:, None] // HGRN_CHUNK) == (i[None, :] // HGRN_CHUNK))
    return jnp.asarray(m.astype(np.float32), dtype=BF16)


def _hgrn_prompt(rq, rk, rv, lf, rg, g_rec, batch, seq, tt):
    nt = seq // tt
    n = batch * seq
    tile = pl.BlockSpec((tt, REC_WIDTH), lambda b, t: (b * nt + t, 0))
    return pl.pallas_call(
        functools.partial(_hgrn_prompt_kernel, tt=tt),
        grid=(batch, nt),
        in_specs=[pl.BlockSpec((HGRN_CUM_BLOCK, HGRN_CUM_BLOCK), lambda b, t: (0, 0)),
                  tile, tile, tile, tile, tile,
                  pl.BlockSpec((1, REC_WIDTH), lambda b, t: (0, 0))],
        out_specs=[tile, pl.BlockSpec((1, REC_HEADS, REC_DK, REC_DK), lambda b, t: (b, 0, 0, 0))],
        out_shape=[jax.ShapeDtypeStruct((n, REC_WIDTH), BF16),
                   jax.ShapeDtypeStruct((batch, REC_HEADS, REC_DK, REC_DK), F32)],
        scratch_shapes=[pltpu.VMEM((REC_HEADS, REC_DK, REC_DK), F32),
                        pltpu.VMEM((tt, REC_WIDTH), F32),
                        pltpu.VMEM((tt, REC_WIDTH), F32)],
        compiler_params=pltpu.CompilerParams(dimension_semantics=("arbitrary", "arbitrary"),
                                             vmem_limit_bytes=VMEM_LIMIT),
        name="hgrn_prompt",
    )(_hgrn_tri(), rq, rk, rv, lf, rg, g_rec)


def _hgrn_sample_kernel(q_ref, k_ref, v_ref, lf_ref, gate_ref, grec_ref, s0_ref, o_ref, s_ref, *, group):
    rowi = lax.broadcasted_iota(jnp.int32, (group, REC_DK), 0)
    outs = []
    for h in range(REC_HEADS):
        cs = slice(h * REC_DK, (h + 1) * REC_DK)
        q = q_ref[:, cs]
        k = k_ref[:, cs]
        v = v_ref[:, cs]
        ft = jnp.exp(lf_ref[:, cs]).T
        outer = []
        for n in range(group):
            kn = jnp.where(rowi == n, k, jnp.zeros_like(k))
            outer.append(_tn_dot(kn, v))
        new = []
        for n in range(group):
            st = s0_ref[n, h] * ft[:, n:n + 1] + outer[n]
            s_ref[n, h] = st
            new.append(st.astype(BF16))
        o_h = jnp.zeros((group, REC_DK), F32)
        for n in range(group):
            o_h = jnp.where(rowi == n, jnp.dot(q, new[n], preferred_element_type=F32), o_h)
        outs.append(o_h)
    o = jnp.concatenate(outs, axis=1)
    o_ref[...] = (_rms(o, grec_ref[...]) * gate_ref[...].astype(F32)).astype(BF16)


def _hgrn_sample(rq, rk, rv, lf, rg, g_rec, s0, group):
    n = rq.shape[0]
    tile = pl.BlockSpec((group, REC_WIDTH), lambda i: (i, 0))
    st = pl.BlockSpec((group, REC_HEADS, REC_DK, REC_DK), lambda i: (i, 0, 0, 0))
    return pl.pallas_call(
        functools.partial(_hgrn_sample_kernel, group=group),
        grid=(n // group,),
        in_specs=[tile, tile, tile, tile, tile, pl.BlockSpec((1, REC_WIDTH), lambda i: (0, 0)), st],
        out_specs=[tile, st],
        out_shape=[jax.ShapeDtypeStruct((n, REC_WIDTH), BF16),
                   jax.ShapeDtypeStruct(s0.shape, F32)],
        compiler_params=pltpu.CompilerParams(dimension_semantics=("arbitrary",),
                                             vmem_limit_bytes=VMEM_LIMIT),
        name="hgrn_sample",
    )(rq, rk, rv, lf, rg, g_rec, s0)


FF_BLOCK = 1024
TAIL_VMEM_LIMIT = V7X_VMEM_BYTES * 7 // 8


def _tail_kernel(x_ref, att_ref, rec_ref, p_ref, wo_ref, gffn_ref, wup_ref, wdn_ref,
                 wg_ref, wp_ref, gfin_ref, y_ref, *, final):
    mix = (jnp.dot(att_ref[...], wo_ref[:ATT_WIDTH, :], preferred_element_type=F32)
           + jnp.dot(rec_ref[...], wo_ref[ATT_WIDTH:, :], preferred_element_type=F32))
    x = x_ref[...] + mix
    h = _rms(x, gffn_ref[...]).astype(BF16)
    for c0 in range(0, D_FF, FF_BLOCK):
        up = jnp.dot(h, wup_ref[:, c0:c0 + FF_BLOCK], preferred_element_type=F32)
        act = jnp.square(jnp.maximum(up, 0.0)).astype(BF16)
        x = x + jnp.dot(act, wdn_ref[c0:c0 + FF_BLOCK, :], preferred_element_type=F32)
    gate = jax.nn.sigmoid(jnp.dot(x.astype(BF16), wg_ref[...], preferred_element_type=F32))
    emb = jnp.dot(p_ref[...].astype(BF16), wp_ref[...], preferred_element_type=F32)
    x = x + gate * emb
    y_ref[...] = _rms(x, gfin_ref[...]) if final else x


def _tail(x, att, rec, p, w_out, g_ffn, w_up, w_down, w_gate, w_proj, g_final, final, tm):
    n = x.shape[0]
    row = lambda i: (i, 0)
    const = lambda i: (0, 0)

    def resident(a):
        return pl.BlockSpec(a.shape, const, pipeline_mode=pl.Buffered(1))

    return pl.pallas_call(
        functools.partial(_tail_kernel, final=final),
        grid=(n // tm,),
        in_specs=[pl.BlockSpec((tm, D_MODEL), row),
                  pl.BlockSpec((tm, ATT_WIDTH), row),
                  pl.BlockSpec((tm, REC_WIDTH), row),
                  pl.BlockSpec((tm, PLE_DIM), row),
                  resident(w_out), resident(g_ffn), resident(w_up), resident(w_down),
                  resident(w_gate), resident(w_proj), resident(g_final)],
        out_specs=pl.BlockSpec((tm, D_MODEL), row),
        out_shape=jax.ShapeDtypeStruct((n, D_MODEL), F32),
        compiler_params=pltpu.CompilerParams(dimension_semantics=("arbitrary",),
                                             vmem_limit_bytes=TAIL_VMEM_LIMIT),
        name="tail",
    )(x, att, rec, p, w_out, g_ffn, w_up, w_down, w_gate, w_proj, g_final)


def kernel(x_prompt, x_sample, cache_k, cache_v, state_hgrn, page_table, p_prompt, p_sample,
           w_in, lambda_q1, lambda_k1, lambda_q2, lambda_k2, g_subln, hgrn_lb, g_rec, w_out,
           g_mix, g_ffn, w_up, w_down, w_ple_gate, w_ple_proj, g_final):
    batch, seq, d = x_prompt.shape
    n_seq = x_sample.shape[0]
    depth = w_in.shape[0]
    assert d == D_MODEL and w_in.shape[2] == N_PROJ * ATT_WIDTH and x_sample.shape[1] == 1
    assert cache_k.shape[3:] == (ATT_HEADS, HEAD_W) and state_hgrn.shape[2:] == (REC_HEADS, REC_DK, REC_DK)
    n_p = batch * seq

    hp = x_prompt.reshape(n_p, d)
    hs = x_sample.reshape(n_seq, d)
    row2 = lambda a: a.reshape(1, -1)
    outs = [[] for _ in range(6)]
    for li in range(depth):
        last = li == depth - 1
        w_in_b = w_in[li].astype(BF16)
        w_out_b = w_out[li].astype(BF16)
        w_up_b = w_up[li].astype(BF16)
        w_dn_b = w_down[li].astype(BF16)
        w_g_b = w_ple_gate[li].astype(BF16)
        w_p_b = w_ple_proj[li].astype(BF16)
        lams = jnp.stack([lambda_q1[li], lambda_k1[li], lambda_q2[li], lambda_k2[li]], axis=0)
        g_sub = row2(g_subln[li])
        g_r = row2(g_rec[li])
        tail_w = (w_out_b, row2(g_ffn[li]), w_up_b, w_dn_b, w_g_b, w_p_b, row2(g_final))

        aq, ak, av, akb, avb, rq, lf, rk, rv, rg = _inproj(hp, row2(g_mix[li]), w_in_b, hgrn_lb, li, tm=1024)
        att = _attn_prompt(lams, aq, akb, avb, g_sub, li, batch, seq, tq=256, heads_per_step=2)
        rec, s_p = _hgrn_prompt(rq, rk, rv, lf, rg, g_r, batch, seq, tt=1024)
        hp = _tail(hp, att, rec, p_prompt[li].reshape(n_p, PLE_DIM), *tail_w, final=last, tm=1024)
        outs[0].append(ak.reshape(batch, seq, ATT_HEADS, HEAD_W))
        outs[1].append(av.reshape(batch, seq, ATT_HEADS, HEAD_W))
        outs[2].append(s_p)

        aq, ak, av, _, _, rq, lf, rk, rv, rg = _inproj(hs, row2(g_mix[li]), w_in_b, hgrn_lb, li, tm=n_seq)
        att = _attn_sample(page_table, lams, aq, ak, av, g_sub, cache_k, cache_v, li, group=2)
        rec, s_s = _hgrn_sample(rq, rk, rv, lf, rg, g_r, state_hgrn[li], group=16)
        hs = _tail(hs, att, rec, p_sample[li].reshape(n_seq, PLE_DIM), *tail_w, final=last, tm=n_seq)
        outs[3].append(ak.reshape(n_seq, 1, ATT_HEADS, HEAD_W))
        outs[4].append(av.reshape(n_seq, 1, ATT_HEADS, HEAD_W))
        outs[5].append(s_s)

    k_p, v_p, s_p, k_s, v_s, s_s = [jnp.stack(o, axis=0) for o in outs]
    return (hp.reshape(batch, seq, d), hs.reshape(n_seq, 1, d), k_p, v_p, s_p, k_s, v_s, s_s)
```
